```python
import jax
import jax.numpy as jnp
from jax import lax
import numpy as np

D_MODEL = 1024
BATCH = 2
SEQ = 8192
DEPTH = 4
DEC_BATCH = 32
DEC_SEQ = 1
PAST_LEN = 8192
PAGE_SIZE = 128

N_HEADS = 16
HEAD_DIM = D_MODEL // N_HEADS
SB_SCALE = HEAD_DIM ** -0.5
SB_BIAS_INIT = -6.0
Q_BLOCK = 128
CONV_WIDTH = 31
D_FF = 2816
N_EXPERTS = 8
TOP_K = 2
D_FF_EXPERT = D_FF
D_PLE = 256
RMS_EPS = 1e-6
LN_EPS = 1e-5
N_CONV_LAYERS = (DEPTH + 1) // 2
N_ATTN_LAYERS = DEPTH // 2
N_DENSE_LAYERS = (DEPTH + 1) // 2
N_MOE_LAYERS = DEPTH // 2

kernel_name = 'hybrid_conformer_conv_stick_breaking_decoder_step'


def rms_norm(x, g):
    xf = x.astype(jnp.float32)
    y = xf * lax.rsqrt(jnp.mean(xf * xf, axis=-1, keepdims=True) + RMS_EPS)
    return (y * g.astype(jnp.float32)).astype(x.dtype)


def layer_norm(x, g, b):
    xf = x.astype(jnp.float32)
    xc = xf - jnp.mean(xf, axis=-1, keepdims=True)
    var = jnp.mean(xc * xc, axis=-1, keepdims=True)
    y = xc * lax.rsqrt(var + LN_EPS) * g.astype(jnp.float32) + b.astype(jnp.float32)
    return y.astype(x.dtype)


def conv_module(u, hist, w_pw1, b_pw1, w_dw, b_dw, ln_g, ln_b, w_pw2, b_pw2):
    a = u @ w_pw1 + b_pw1
    glu = a[..., :D_MODEL] * jax.nn.sigmoid(a[..., D_MODEL:])
    buf = jnp.concatenate([hist.astype(glu.dtype), glu], axis=1)
    y = lax.conv_general_dilated(
        buf, w_dw[:, None, :].astype(buf.dtype), window_strides=(1,), padding='VALID',
        dimension_numbers=('NWC', 'WIO', 'NWC'), feature_group_count=D_MODEL) + b_dw
    y = layer_norm(y, ln_g, ln_b)
    y = y * jax.nn.sigmoid(y)
    out = y @ w_pw2 + b_pw2
    return out, buf[:, buf.shape[1] - (CONV_WIDTH - 1):]


def qkv_heads(u, w_qkv):
    b, t, _ = u.shape
    qkv = (u @ w_qkv).reshape(b, t, 3, N_HEADS, HEAD_DIM)
    return qkv[:, :, 0], qkv[:, :, 1], qkv[:, :, 2]


def sb_attend(q, q_pos, k, v, k_pos, sb_bias):
    z = jnp.einsum('bqhd,bkhd->bhqk', q, k, preferred_element_type=jnp.float32) * SB_SCALE
    z = z + sb_bias.astype(jnp.float32)[None, :, None, None]
    causal = k_pos[None, :] < q_pos[:, None]
    log_keep = jnp.where(causal, -jax.nn.softplus(z), 0.0)
    suffix = lax.cumsum(log_keep, axis=3, reverse=True) - log_keep
    a = jnp.where(causal, jnp.exp(jax.nn.log_sigmoid(z) + suffix), 0.0)
    return jnp.einsum('bhqk,bkhd->bqhd', a.astype(v.dtype), v)


def stick_breaking_prompt(q, k, v, sb_bias):
    b, s, h, d = q.shape
    n_blk = s // Q_BLOCK
    pos = jnp.arange(s, dtype=jnp.int32)
    q_blocks = q.reshape(b, n_blk, Q_BLOCK, h, d).swapaxes(0, 1)
    pos_blocks = pos.reshape(n_blk, Q_BLOCK)
    o = lax.map(lambda xs: sb_attend(xs[0], xs[1], k, v, pos, sb_bias), (q_blocks, pos_blocks))
    return o.swapaxes(0, 1).reshape(b, s, h * d)


def gather_pages(pool, page_table):
    pages = pool[page_table]
    nb, npg, ps, h, d = pages.shape
    return pages.reshape(nb, npg * ps, h, d)


def stick_breaking_sample(q, k_new, v_new, past_k, past_v, sb_bias):
    b, t, h, d = q.shape
    past = past_k.shape[1]
    k_all = jnp.concatenate([past_k.astype(k_new.dtype), k_new], axis=1)
    v_all = jnp.concatenate([past_v.astype(v_new.dtype), v_new], axis=1)
    k_pos = jnp.arange(past + t, dtype=jnp.int32)
    q_pos = past + jnp.arange(t, dtype=jnp.int32)
    return sb_attend(q, q_pos, k_all, v_all, k_pos, sb_bias).reshape(b, t, h * d)


def swiglu(x, wg, wu, wd):
    return (jax.nn.silu(x @ wg) * (x @ wu)) @ wd


def moe_swiglu(x, w_router, wg, wu, wd):
    logits = jnp.einsum('btd,de->bte', x, w_router, preferred_element_type=jnp.float32)
    top_val, top_idx = lax.top_k(logits, TOP_K)
    top_w = jax.nn.softmax(top_val, axis=-1)
    gate = jnp.sum(jax.nn.one_hot(top_idx, N_EXPERTS, dtype=jnp.float32) * top_w[..., None], axis=-2)
    out = jnp.zeros_like(x)
    for e in range(N_EXPERTS):
        out = out + gate[..., e:e + 1].astype(x.dtype) * swiglu(x, wg[e], wu[e], wd[e])
    return out


def per_layer_embed(h, p, g_norm, w_gate, w_proj):
    gate = jax.nn.sigmoid(rms_norm(h, g_norm) @ w_gate)
    return h + gate * (p @ w_proj)


def setup_inputs(seed: int = 0) -> dict:
    key = jax.random.key(seed)
    ks = jax.random.split(key, 40)

    def nrm(k, shape, scale):
        return jax.random.normal(k, shape, jnp.float32) * scale

    n_pages = PAST_LEN // PAGE_SIZE
    n_used = DEC_BATCH * n_pages
    n_pool = n_used + max(1, n_used // 4)
    page_table = jax.random.permutation(ks[0], n_pool)[:n_used].reshape(DEC_BATCH, n_pages).astype(jnp.int32)
    D = D_MODEL
    return {
        'x_prompt': nrm(ks[1], (BATCH, SEQ, D), 1.0),
        'x_sample': nrm(ks[2], (DEC_BATCH, DEC_SEQ, D), 1.0),
        'cache_k': nrm(ks[3], (N_ATTN_LAYERS, n_pool, PAGE_SIZE, N_HEADS, HEAD_DIM), 1.0),
        'cache_v': nrm(ks[4], (N_ATTN_LAYERS, n_pool, PAGE_SIZE, N_HEADS, HEAD_DIM), 1.0),
        'state_conv': nrm(ks[5], (N_CONV_LAYERS, DEC_BATCH, CONV_WIDTH - 1, D), 0.5),
        'page_table': page_table,
        'p_prompt': nrm(ks[6], (DEPTH, BATCH, SEQ, D_PLE), 1.0),
        'p_sample': nrm(ks[7], (DEPTH, DEC_BATCH, DEC_SEQ, D_PLE), 1.0),
        'norm_mix': 1.0 + nrm(ks[8], (DEPTH, D), 0.05),
        'w_pw1': nrm(ks[9], (N_CONV_LAYERS, D, 2 * D), D ** -0.5),
        'b_pw1': nrm(ks[10], (N_CONV_LAYERS, 2 * D), 0.02),
        'w_dw': nrm(ks[11], (N_CONV_LAYERS, CONV_WIDTH, D), CONV_WIDTH ** -0.5),
        'b_dw': nrm(ks[12], (N_CONV_LAYERS, D), 0.02),
        'ln_conv_g': 1.0 + nrm(ks[13], (N_CONV_LAYERS, D), 0.05),
        'ln_conv_b': nrm(ks[14], (N_CONV_LAYERS, D), 0.02),
        'w_pw2': nrm(ks[15], (N_CONV_LAYERS, D, D), D ** -0.5),
        'b_pw2': nrm(ks[16], (N_CONV_LAYERS, D), 0.02),
        'w_qkv': nrm(ks[17], (N_ATTN_LAYERS, D, 3 * D), D ** -0.5),
        'w_o': nrm(ks[18], (N_ATTN_LAYERS, D, D), D ** -0.5),
        'sb_bias': SB_BIAS_INIT + nrm(ks[31], (N_ATTN_LAYERS, N_HEADS), 0.3),
        'norm_ffn': 1.0 + nrm(ks[19], (DEPTH, D), 0.05),
        'w_ffn_gate': nrm(ks[20], (N_DENSE_LAYERS, D, D_FF), D ** -0.5),
        'w_ffn_up': nrm(ks[21], (N_DENSE_LAYERS, D, D_FF), D ** -0.5),
        'w_ffn_down': nrm(ks[22], (N_DENSE_LAYERS, D_FF, D), D_FF ** -0.5),
        'w_router': nrm(ks[23], (N_MOE_LAYERS, D, N_EXPERTS), D ** -0.5),
        'w_moe_gate': nrm(ks[24], (N_MOE_LAYERS, N_EXPERTS, D, D_FF_EXPERT), D ** -0.5),
        'w_moe_up': nrm(ks[25], (N_MOE_LAYERS, N_EXPERTS, D, D_FF_EXPERT), D ** -0.5),
        'w_moe_down': nrm(ks[26], (N_MOE_LAYERS, N_EXPERTS, D_FF_EXPERT, D), D_FF_EXPERT ** -0.5),
        'norm_ple': 1.0 + nrm(ks[27], (DEPTH, D), 0.05),
        'w_ple_gate': nrm(ks[28], (DEPTH, D, D), D ** -0.5),
        'w_ple_proj': nrm(ks[29], (DEPTH, D_PLE, D), D_PLE ** -0.5),
        'norm_final': 1.0 + nrm(ks[30], (D,), 0.05),
    }


def reference(x_prompt, x_sample, cache_k, cache_v, state_conv, page_table, p_prompt, p_sample,
              norm_mix, w_pw1, b_pw1, w_dw, b_dw, ln_conv_g, ln_conv_b, w_pw2, b_pw2,
              w_qkv, w_o, sb_bias, norm_ffn, w_ffn_gate, w_ffn_up, w_ffn_down,
              w_router, w_moe_gate, w_moe_up, w_moe_down,
              norm_ple, w_ple_gate, w_ple_proj, norm_final):
    hp, hs = x_prompt, x_sample
    k_prompt_rows, v_prompt_rows, k_sample_rows, v_sample_rows = [], [], [], []
    conv_prompt_states, conv_sample_states = [], []
    for i in range(DEPTH):
        j = i // 2
        up = rms_norm(hp, norm_mix[i])
        us = rms_norm(hs, norm_mix[i])
        if i % 2 == 0:
            params = (w_pw1[j], b_pw1[j], w_dw[j], b_dw[j], ln_conv_g[j], ln_conv_b[j], w_pw2[j], b_pw2[j])
            hist0 = jnp.zeros((up.shape[0], CONV_WIDTH - 1, D_MODEL), up.dtype)
            mp, st_p = conv_module(up, hist0, *params)
            ms, st_s = conv_module(us, state_conv[j], *params)
            conv_prompt_states.append(st_p)
            conv_sample_states.append(st_s)
        else:
            qp, kp, vp = qkv_heads(up, w_qkv[j])
            mp = stick_breaking_prompt(qp, kp, vp, sb_bias[j]) @ w_o[j]
            qs, kn, vn = qkv_heads(us, w_qkv[j])
            past_k = gather_pages(cache_k[j], page_table)
            past_v = gather_pages(cache_v[j], page_table)
            ms = stick_breaking_sample(qs, kn, vn, past_k, past_v, sb_bias[j]) @ w_o[j]
            k_prompt_rows.append(kp)
            v_prompt_rows.append(vp)
            k_sample_rows.append(kn)
            v_sample_rows.append(vn)
        hp = hp + mp
        hs = hs + ms
        fp = rms_norm(hp, norm_ffn[i])
        fs = rms_norm(hs, norm_ffn[i])
        if i % 2 == 0:
            hp = hp + swiglu(fp, w_ffn_gate[j], w_ffn_up[j], w_ffn_down[j])
            hs = hs + swiglu(fs, w_ffn_gate[j], w_ffn_up[j], w_ffn_down[j])
        else:
            hp = hp + moe_swiglu(fp, w_router[j], w_moe_gate[j], w_moe_up[j], w_moe_down[j])
            hs = hs + moe_swiglu(fs, w_router[j], w_moe_gate[j], w_moe_up[j], w_moe_down[j])
        hp = per_layer_embed(hp, p_prompt[i], norm_ple[i], w_ple_gate[i], w_ple_proj[i])
        hs = per_layer_embed(hs, p_sample[i], norm_ple[i], w_ple_gate[i], w_ple_proj[i])
    y_prompt = rms_norm(hp, norm_final)
    y_sample = rms_norm(hs, norm_final)
    return (y_prompt, y_sample,
            jnp.stack(k_prompt_rows), jnp.stack(v_prompt_rows),
            jnp.stack(k_sample_rows), jnp.stack(v_sample_rows),
            jnp.stack(conv_prompt_states), jnp.stack(conv_sample_states))
```

```python
import functools

import jax
import jax.numpy as jnp
from jax import lax
from jax.experimental import pallas as pl
from jax.experimental.pallas import tpu as pltpu

D_MODEL = 1024
N_HEADS = 16
HEAD_DIM = 64
SB_SCALE = HEAD_DIM ** -0.5
CONV_WIDTH = 31
D_FF = 2816
N_EXPERTS = 8
D_PLE = 256
RMS_EPS = 1e-6
LN_EPS = 1e-5
PAGE_SIZE = 128

LANES = 128
HALO_ROWS = 32
PROMPT_ROWS = 512
CONV_ROWS = 256
ATTN_ROWS = 256
FF_CHUNK = 256
MOE_CHUNKS = 2
DEC_PAGES = 8
VMEM_LIMIT = 56 * 1024 * 1024

BF16 = jnp.bfloat16
F32 = jnp.float32


def _params(n_axes):
    return pltpu.CompilerParams(dimension_semantics=("arbitrary",) * n_axes,
                                vmem_limit_bytes=VMEM_LIMIT)


def _rms(x, g):
    ms = jnp.mean(x * x, axis=-1, keepdims=True)
    return x * lax.rsqrt(ms + RMS_EPS) * g


def _sigmoid(x):
    return 1.0 / (1.0 + jnp.exp(-x))


def _const_spec(shape):
    zeros = (0,) * len(shape)
    return pl.BlockSpec(shape, lambda *_: zeros)


def _glu_kernel(h_ref, g_ref, w_ref, b_ref, o_ref):
    u = _rms(h_ref[...], g_ref[...]).astype(BF16)
    a = jnp.dot(u, w_ref[...], preferred_element_type=F32) + b_ref[...]
    o_ref[...] = a[:, :D_MODEL] * _sigmoid(a[:, D_MODEL:])


def _glu(h, g, w, b, tm):
    m = h.shape[0]
    return pl.pallas_call(
        _glu_kernel,
        grid=(m // tm,),
        in_specs=[pl.BlockSpec((tm, D_MODEL), lambda i: (i, 0)),
                  _const_spec((1, D_MODEL)),
                  _const_spec((D_MODEL, 2 * D_MODEL)),
                  _const_spec((1, 2 * D_MODEL))],
        out_specs=pl.BlockSpec((tm, D_MODEL), lambda i: (i, 0)),
        out_shape=jax.ShapeDtypeStruct((m, D_MODEL), F32),
        compiler_params=_params(1),
        name="glu",
    )(h, g, w, b)


def _ln_swish(y, lg, lb):
    mu = jnp.mean(y, axis=-1, keepdims=True)
    yc = y - mu
    var = jnp.mean(yc * yc, axis=-1, keepdims=True)
    yn = yc * lax.rsqrt(var + LN_EPS) * lg + lb
    return yn * _sigmoid(yn)


def _dwconv_kernel(halo_ref, x_ref, wd_ref, bd_ref, lg_ref, lb_ref, o_ref, buf, sh, ybuf):
    tq = CONV_ROWS
    i = pl.program_id(1)
    buf[0:HALO_ROWS, :] = jnp.where(i > 0, halo_ref[...], 0.0)
    buf[HALO_ROWS:HALO_ROWS + tq, :] = x_ref[...]
    base = HALO_ROWS - (CONV_WIDTH - 1)
    n_taps = [len(range(r, CONV_WIDTH, 8)) for r in range(8)]
    for r in range(8):
        n = tq + 8 * (n_taps[r] - 1)
        sh[r, 0:n, :] = buf[base + r:base + r + n, :]

    rb = 64

    def col_body(c, carry):
        cs = pl.ds(pl.multiple_of(c * LANES, LANES), LANES)
        for blk in range(tq // rb):
            acc = jnp.zeros((rb, LANES), F32) + bd_ref[:, cs]
            for r in range(8):
                for a in range(n_taps[r]):
                    w = 8 * a + r
                    lo = 8 * a + blk * rb
                    acc = acc + sh[r, lo:lo + rb, cs] * wd_ref[w:w + 1, cs]
            ybuf[blk * rb:(blk + 1) * rb, cs] = acc
        return carry

    lax.fori_loop(0, D_MODEL // LANES, col_body, 0)
    o_ref[...] = _ln_swish(ybuf[...], lg_ref[...], lb_ref[...]).astype(BF16)


def _dwconv_prompt(glu, n_seq, seq, wd, bd, lg, lb):
    tq = CONV_ROWS
    nq = seq // tq
    halo_per_blk = tq // HALO_ROWS

    def halo_map(b, i):
        return (jnp.maximum(b * (seq // HALO_ROWS) + i * halo_per_blk - 1, 0), 0)

    return pl.pallas_call(
        _dwconv_kernel,
        grid=(n_seq, nq),
        in_specs=[pl.BlockSpec((HALO_ROWS, D_MODEL), halo_map),
                  pl.BlockSpec((tq, D_MODEL), lambda b, i: (b * nq + i, 0)),
                  _const_spec((CONV_WIDTH, D_MODEL)),
                  _const_spec((1, D_MODEL)),
                  _const_spec((1, D_MODEL)),
                  _const_spec((1, D_MODEL))],
        out_specs=pl.BlockSpec((tq, D_MODEL), lambda b, i: (b * nq + i, 0)),
        out_shape=jax.ShapeDtypeStruct((n_seq * seq, D_MODEL), BF16),
        scratch_shapes=[pltpu.VMEM((HALO_ROWS + tq, D_MODEL), F32),
                        pltpu.VMEM((8, tq + 24, D_MODEL), F32),
                        pltpu.VMEM((tq, D_MODEL), F32)],
        compiler_params=_params(2),
        name="dwconv_prompt",
    )(glu, glu, wd, bd, lg, lb)


def _dwconv_sample_kernel(hist_ref, x_ref, wd_ref, bd_ref, lg_ref, lb_ref, o_ref):
    acc = x_ref[...] * wd_ref[CONV_WIDTH - 1:CONV_WIDTH, :] + bd_ref[...]
    for w in range(CONV_WIDTH - 1):
        acc = acc + hist_ref[w] * wd_ref[w:w + 1, :]
    o_ref[...] = _ln_swish(acc, lg_ref[...], lb_ref[...]).astype(BF16)


def _dwconv_sample(hist_t, glu, wd, bd, lg, lb):
    n = glu.shape[0]
    return pl.pallas_call(
        _dwconv_sample_kernel,
        grid=(1,),
        in_specs=[_const_spec((CONV_WIDTH - 1, n, D_MODEL)),
                  _const_spec((n, D_MODEL)),
                  _const_spec((CONV_WIDTH, D_MODEL)),
                  _const_spec((1, D_MODEL)),
                  _const_spec((1, D_MODEL)),
                  _const_spec((1, D_MODEL))],
        out_specs=_const_spec((n, D_MODEL)),
        out_shape=jax.ShapeDtypeStruct((n, D_MODEL), BF16),
        compiler_params=_params(1),
        name="dwconv_sample",
    )(hist_t, glu, wd, bd, lg, lb)


def _qkv_kernel(h_ref, g_ref, w_ref, kf_ref, vf_ref, qb_ref, kb_ref, vb_ref):
    u = _rms(h_ref[...], g_ref[...]).astype(BF16)
    qkv = jnp.dot(u, w_ref[...], preferred_element_type=F32)
    k = qkv[:, D_MODEL:2 * D_MODEL]
    v = qkv[:, 2 * D_MODEL:]
    kf_ref[...] = k
    vf_ref[...] = v
    qb_ref[...] = (qkv[:, :D_MODEL] * SB_SCALE).astype(BF16)
    kb_ref[...] = k.astype(BF16)
    vb_ref[...] = v.astype(BF16)


def _qkv(h, g, w, tm):
    m = h.shape[0]
    row = pl.BlockSpec((tm, D_MODEL), lambda i: (i, 0))
    return pl.pallas_call(
        _qkv_kernel,
        grid=(m // tm,),
        in_specs=[row, _const_spec((1, D_MODEL)), _const_spec((D_MODEL, 3 * D_MODEL))],
        out_specs=[row] * 5,
        out_shape=[jax.ShapeDtypeStruct((m, D_MODEL), F32)] * 2
        + [jax.ShapeDtypeStruct((m, D_MODEL), BF16)] * 3,
        compiler_params=_params(1),
        name="qkv",
    )(h, g, w)


def _softplus(z):
    return jnp.maximum(z, 0.0) + jnp.log(1.0 + jnp.exp(-jnp.abs(z)))


def _sb_tile(qh, kblk, vblk, bias, carry, tmat, mask):
    z = lax.dot_general(qh, kblk, (((1,), (1,)), ((), ())), preferred_element_type=F32) + bias
    sp = _softplus(z)
    if mask is not None:
        sp = jnp.where(mask, sp, 0.0)
    cum = jnp.dot(sp.astype(BF16), tmat, preferred_element_type=F32)
    a = jnp.exp(z - cum - carry)
    if mask is not None:
        a = jnp.where(mask, a, 0.0)
    pv = jnp.dot(a.astype(BF16), vblk, preferred_element_type=F32)
    return pv, jnp.sum(sp, axis=1, keepdims=True)


def _attn_kernel(bias_ref, q_ref, k_ref, v_ref, t_ref, o_ref, o_acc, c_acc):
    t = ATTN_ROWS
    hp = pl.program_id(1)
    i = pl.program_id(2)
    q = q_ref[...]
    lane = lax.broadcasted_iota(jnp.int32, (1, LANES), 1)
    first = lane < HEAD_DIM
    qs = (jnp.where(first, q, jnp.zeros_like(q)), jnp.where(first, jnp.zeros_like(q), q))
    biases = (bias_ref[2 * hp], bias_ref[2 * hp + 1])
    tmat = t_ref[...]
    row = lax.broadcasted_iota(jnp.int32, (t, t), 0)
    col = lax.broadcasted_iota(jnp.int32, (t, t), 1)
    causal = col < row

    def tile(j, mask):
        ks = pl.ds(pl.multiple_of(j * t, t), t)
        kblk = k_ref[ks, :]
        vblk = v_ref[ks, :]
        for hh in range(2):
            pv, tot = _sb_tile(qs[hh], kblk, vblk, biases[hh], c_acc[hh], tmat, mask)
            o_acc[hh] += pv
            c_acc[hh] += tot

    o_acc[...] = jnp.zeros_like(o_acc)
    c_acc[...] = jnp.zeros_like(c_acc)
    tile(i, causal)

    def body(s, carry):
        tile(i - s, None)
        return carry

    lax.fori_loop(1, i + 1, body, 0)
    o_ref[...] = jnp.where(first, o_acc[0], o_acc[1]).astype(BF16)


def _attn_prompt(sb_bias, qb, kb, vb, n_seq, seq):
    t = ATTN_ROWS
    nq = seq // t
    tmat = (jnp.arange(t)[:, None] >= jnp.arange(t)[None, :]).astype(BF16)
    k3 = kb.reshape(n_seq, seq, D_MODEL)
    v3 = vb.reshape(n_seq, seq, D_MODEL)
    kv_spec = pl.BlockSpec((None, seq, LANES), lambda b, hp, i: (b, 0, hp))
    return pl.pallas_call(
        _attn_kernel,
        grid=(n_seq, N_HEADS // 2, nq),
        in_specs=[pl.BlockSpec(memory_space=pltpu.SMEM),
                  pl.BlockSpec((t, LANES), lambda b, hp, i: (b * nq + i, hp)),
                  kv_spec, kv_spec,
                  _const_spec((t, t))],
        out_specs=pl.BlockSpec((t, LANES), lambda b, hp, i: (b * nq + i, hp)),
        out_shape=jax.ShapeDtypeStruct((n_seq * seq, D_MODEL), BF16),
        scratch_shapes=[pltpu.VMEM((2, t, LANES), F32), pltpu.VMEM((2, t, 1), F32)],
        compiler_params=_params(3),
        name="attn_prompt",
    )(sb_bias, qb, k3, v3, tmat)


def _decode_kernel(pt_ref, qm_ref, bias_ref, tl_ref, ex_ref, *refs):
    k_refs = refs[:DEC_PAGES]
    v_refs = refs[DEC_PAGES:2 * DEC_PAGES]
    o_ref, acc, carry = refs[2 * DEC_PAGES:]
    g = pl.program_id(1)

    @pl.when(g == 0)
    def _():
        acc[...] = jnp.zeros_like(acc)
        carry[...] = jnp.zeros_like(carry)

    qm = qm_ref[...]
    for s in range(DEC_PAGES):
        z = jnp.dot(k_refs[s][...].astype(BF16), qm, preferred_element_type=F32) + bias_ref[...]
        sp = _softplus(z)
        cum = jnp.dot(tl_ref[...], sp.astype(BF16), preferred_element_type=F32)
        a = jnp.exp(z - cum - carry[...])
        carry[...] += cum[0:1, :]
        a_wide = jnp.dot(a.astype(BF16), ex_ref[...], preferred_element_type=F32)
        prod = a_wide * v_refs[s][...]
        acc[...] += jnp.sum(prod.reshape(PAGE_SIZE // 8, 8, D_MODEL), axis=0)

    @pl.when(g == pl.num_programs(1) - 1)
    def _():
        o_ref[...] = jnp.sum(acc[...], axis=0, keepdims=True)


def _attn_decode(page_table, q_scaled, sb_bias, cache_k, cache_v):
    n, n_pages = page_table.shape
    head_of_lane = jnp.arange(D_MODEL) // HEAD_DIM
    sel = head_of_lane[:, None] == jnp.arange(LANES)[None, :]
    qm = jnp.where(sel[None], q_scaled[:, :, None], 0.0).astype(BF16)
    bias = jnp.zeros((1, LANES), F32).at[0, :N_HEADS].set(sb_bias)
    tl = (jnp.arange(PAGE_SIZE)[None, :] >= jnp.arange(PAGE_SIZE)[:, None]).astype(BF16)
    ex = sel.T.astype(BF16)
    steps = n_pages // DEC_PAGES

    def page_spec(s):
        def index_map(b, g, pt):
            return (pt[b, n_pages - 1 - g * DEC_PAGES - s], 0, 0)
        return pl.BlockSpec((None, PAGE_SIZE, D_MODEL), index_map)

    grid_spec = pltpu.PrefetchScalarGridSpec(
        num_scalar_prefetch=1,
        grid=(n, steps),
        in_specs=[pl.BlockSpec((None, D_MODEL, LANES), lambda b, g, pt: (b, 0, 0)),
                  pl.BlockSpec((1, LANES), lambda b, g, pt: (0, 0)),
                  pl.BlockSpec((PAGE_SIZE, PAGE_SIZE), lambda b, g, pt: (0, 0)),
                  pl.BlockSpec((LANES, D_MODEL), lambda b, g, pt: (0, 0))]
        + [page_spec(s) for s in range(DEC_PAGES)] * 2,
        out_specs=pl.BlockSpec((None, 1, D_MODEL), lambda b, g, pt: (b, 0, 0)),
        scratch_shapes=[pltpu.VMEM((8, D_MODEL), F32), pltpu.VMEM((1, LANES), F32)],
    )
    out = pl.pallas_call(
        _decode_kernel,
        grid_spec=grid_spec,
        out_shape=jax.ShapeDtypeStruct((n, 1, D_MODEL), F32),
        compiler_params=_params(2),
        name="attn_decode",
    )(page_table, qm, bias, tl, ex, *([cache_k] * DEC_PAGES), *([cache_v] * DEC_PAGES))
    return out.reshape(n, D_MODEL)


def _ple_tail(hres, p_ref, gple_ref, wgate_ref, wproj_ref, gfin_ref, final):
    r = _rms(hres, gple_ref[...]).astype(BF16)
    gate = _sigmoid(jnp.dot(r, wgate_ref[...], preferred_element_type=F32))
    proj = jnp.dot(p_ref[...].astype(BF16), wproj_ref[...], preferred_element_type=F32)
    out = hres + gate * proj
    if final:
        out = _rms(out, gfin_ref[...])
    return out


def _mix_in(h_ref, x_ref, wmix_ref, bmix_ref):
    return h_ref[...] + jnp.dot(x_ref[...].astype(BF16), wmix_ref[...],
                                preferred_element_type=F32) + bmix_ref[...]


def _silu_mul(g, u):
    return (g * _sigmoid(g) * u).astype(BF16)


def _mixer_dense_kernel(h_ref, x_ref, wmix_ref, bmix_ref, gffn_ref, wg_ref, wu_ref, wd_ref,
                        p_ref, gple_ref, wgate_ref, wproj_ref, gfin_ref, o_ref, *, final):
    h1 = _mix_in(h_ref, x_ref, wmix_ref, bmix_ref)
    f = _rms(h1, gffn_ref[...]).astype(BF16)
    acc = h1
    for c in range(D_FF // FF_CHUNK):
        cs = slice(c * FF_CHUNK, (c + 1) * FF_CHUNK)
        g = jnp.dot(f, wg_ref[:, cs], preferred_element_type=F32)
        u = jnp.dot(f, wu_ref[:, cs], preferred_element_type=F32)
        acc = acc + jnp.dot(_silu_mul(g, u), wd_ref[cs, :], preferred_element_type=F32)
    o_ref[...] = _ple_tail(acc, p_ref, gple_ref, wgate_ref, wproj_ref, gfin_ref, final)


def _single(shape):
    zeros = (0,) * len(shape)
    return pl.BlockSpec(shape, lambda *_: zeros, pipeline_mode=pl.Buffered(1))


def _mixer_dense(h, x, wmix, bmix, gffn, wg, wu, wd, p, gple, wgate, wproj, gfin, final, tm):
    m = h.shape[0]
    row = lambda width: pl.BlockSpec((tm, width), lambda i: (i, 0))
    return pl.pallas_call(
        functools.partial(_mixer_dense_kernel, final=final),
        grid=(m // tm,),
        in_specs=[row(D_MODEL), row(D_MODEL),
                  _single((D_MODEL, D_MODEL)), _single((1, D_MODEL)), _single((1, D_MODEL)),
                  _single((D_MODEL, D_FF)), _single((D_MODEL, D_FF)), _single((D_FF, D_MODEL)),
                  row(D_PLE), _single((1, D_MODEL)),
                  _single((D_MODEL, D_MODEL)), _single((D_PLE, D_MODEL)), _single((1, D_MODEL))],
        out_specs=row(D_MODEL),
        out_shape=jax.ShapeDtypeStruct((m, D_MODEL), F32),
        compiler_params=_params(1),
        name="mixer_dense",
    )(h, x, wmix, bmix, gffn, wg, wu, wd, p, gple, wgate, wproj, gfin)


def _top2_gate(logits):
    lane = lax.broadcasted_iota(jnp.int32, logits.shape, 1).astype(F32)
    neg = -jnp.inf
    lg = jnp.where(lane < N_EXPERTS, logits, neg)
    m1 = jnp.max(lg, axis=1, keepdims=True)
    i1 = jnp.min(jnp.where(lg == m1, lane, float(LANES)), axis=1, keepdims=True)
    lg2 = jnp.where(lane == i1, neg, lg)
    m2 = jnp.max(lg2, axis=1, keepdims=True)
    i2 = jnp.min(jnp.where(lg2 == m2, lane, float(LANES)), axis=1, keepdims=True)
    t = jnp.exp(m2 - m1)
    w1 = 1.0 / (1.0 + t)
    w2 = t / (1.0 + t)
    return jnp.where(lane == i1, w1, 0.0) + jnp.where(lane == i2, w2, 0.0)


def _mixer_moe_kernel(h_ref, x_ref, wmix_ref, bmix_ref, gffn_ref, wr_ref, wg_ref, wu_ref, wd_ref,
                      p_ref, gple_ref, wgate_ref, wproj_ref, gfin_ref, o_ref,
                      f_scr, gate_scr, acc_scr, *, final):
    e = pl.program_id(1)
    c = pl.program_id(2)

    @pl.when((e == 0) & (c == 0))
    def _():
        h1 = _mix_in(h_ref, x_ref, wmix_ref, bmix_ref)
        f = _rms(h1, gffn_ref[...])
        logits = jnp.dot(f, wr_ref[...], preferred_element_type=F32,
                         precision=lax.Precision.HIGHEST)
        gate_scr[...] = _top2_gate(logits)
        f_scr[...] = f.astype(BF16)
        acc_scr[...] = h1

    f = f_scr[...]
    g = jnp.dot(f, wg_ref[...], preferred_element_type=F32)
    u = jnp.dot(f, wu_ref[...], preferred_element_type=F32)
    y = jnp.dot(_silu_mul(g, u), wd_ref[...], preferred_element_type=F32)
    gate = gate_scr[...]
    lane = lax.broadcasted_iota(jnp.int32, gate.shape, 1)
    ge = jnp.sum(jnp.where(lane == e, gate, 0.0), axis=1, keepdims=True)
    acc_scr[...] += ge * y

    @pl.when((e == pl.num_programs(1) - 1) & (c == pl.num_programs(2) - 1))
    def _():
        o_ref[...] = _ple_tail(acc_scr[...], p_ref, gple_ref, wgate_ref, wproj_ref, gfin_ref, final)


def _mixer_moe(h, x, wmix, bmix, gffn, wr, wg, wu, wd, p, gple, wgate, wproj, gfin, final, tm):
    m = h.shape[0]
    fc = D_FF // MOE_CHUNKS
    row = lambda width: pl.BlockSpec((tm, width), lambda i, e, c: (i, 0))
    return pl.pallas_call(
        functools.partial(_mixer_moe_kernel, final=final),
        grid=(m // tm, N_EXPERTS, MOE_CHUNKS),
        in_specs=[row(D_MODEL), row(D_MODEL),
                  _single((D_MODEL, D_MODEL)), _single((1, D_MODEL)), _single((1, D_MODEL)),
                  _single((D_MODEL, LANES)),
                  pl.BlockSpec((None, D_MODEL, fc), lambda i, e, c: (e, 0, c)),
                  pl.BlockSpec((None, D_MODEL, fc), lambda i, e, c: (e, 0, c)),
                  pl.BlockSpec((None, fc, D_MODEL), lambda i, e, c: (e, c, 0)),
                  row(D_PLE), _single((1, D_MODEL)),
                  _single((D_MODEL, D_MODEL)), _single((D_PLE, D_MODEL)), _single((1, D_MODEL))],
        out_specs=row(D_MODEL),
        out_shape=jax.ShapeDtypeStruct((m, D_MODEL), F32),
        scratch_shapes=[pltpu.VMEM((tm, D_MODEL), BF16),
                        pltpu.VMEM((tm, LANES), F32),
                        pltpu.VMEM((tm, D_MODEL), F32)],
        compiler_params=_params(3),
        name="mixer_moe",
    )(h, x, wmix, bmix, gffn, wr, wg, wu, wd, p, gple, wgate, wproj, gfin)


def kernel(x_prompt, x_sample, cache_k, cache_v, state_conv, page_table, p_prompt, p_sample, norm_mix, w_pw1, b_pw1, w_dw, b_dw, ln_conv_g, ln_conv_b, w_pw2, b_pw2, w_qkv, w_o, sb_bias, norm_ffn, w_ffn_gate, w_ffn_up, w_ffn_down, w_router, w_moe_gate, w_moe_up, w_moe_down, norm_ple, w_ple_gate, w_ple_proj, norm_final):
    n_seq, seq, _ = x_prompt.shape
    n_dec = x_sample.shape[0]
    depth = norm_mix.shape[0]
    mp = n_seq * seq
    hp = x_prompt.reshape(mp, D_MODEL)
    hs = x_sample.reshape(n_dec, D_MODEL)
    row = lambda a: a.reshape(1, -1)
    zero_bias = jnp.zeros((1, D_MODEL), F32)
    gfin = row(norm_final)
    last_rows = CONV_WIDTH - 1

    k_p, v_p, k_s, v_s, conv_p, conv_s = [], [], [], [], [], []
    for i in range(depth):
        j = i // 2
        final = i == depth - 1
        gmix = row(norm_mix[i])
        ple = (row(norm_ple[i]), w_ple_gate[i].astype(BF16), w_ple_proj[i].astype(BF16), gfin)
        pp = p_prompt[i].reshape(mp, D_PLE)
        ps = p_sample[i].reshape(n_dec, D_PLE)
        if i % 2 == 0:
            w1 = w_pw1[j].astype(BF16)
            b1 = row(b_pw1[j])
            conv_w = (w_dw[j], row(b_dw[j]), row(ln_conv_g[j]), row(ln_conv_b[j]))
            glu_p = _glu(hp, gmix, w1, b1, PROMPT_ROWS)
            glu_s = _glu(hs, gmix, w1, b1, n_dec)
            xp = _dwconv_prompt(glu_p, n_seq, seq, *conv_w)
            xs = _dwconv_sample(jnp.swapaxes(state_conv[j], 0, 1), glu_s, *conv_w)
            conv_p.append(glu_p.reshape(n_seq, seq, D_MODEL)[:, seq - last_rows:])
            conv_s.append(jnp.concatenate([state_conv[j][:, 1:], glu_s[:, None, :]], axis=1))
            mix = (w_pw2[j].astype(BF16), row(b_pw2[j]), row(norm_ffn[i]))
            ffn = (w_ffn_gate[j].astype(BF16), w_ffn_up[j].astype(BF16), w_ffn_down[j].astype(BF16))
            hp = _mixer_dense(hp, xp, *mix, *ffn, pp, *ple, final, PROMPT_ROWS)
            hs = _mixer_dense(hs, xs, *mix, *ffn, ps, *ple, final, n_dec)
        else:
            wq = w_qkv[j].astype(BF16)
            kf, vf, qb, kb, vb = _qkv(hp, gmix, wq, PROMPT_ROWS)
            xp = _attn_prompt(sb_bias[j], qb, kb, vb, n_seq, seq)
            kfs, vfs, qbs, _, _ = _qkv(hs, gmix, wq, n_dec)
            pool = cache_k.shape[1]
            xs = _attn_decode(page_table, qbs.astype(F32), sb_bias[j],
                              cache_k[j].reshape(pool, PAGE_SIZE, D_MODEL),
                              cache_v[j].reshape(pool, PAGE_SIZE, D_MODEL))
            k_p.append(kf.reshape(n_seq, seq, N_HEADS, HEAD_DIM))
            v_p.append(vf.reshape(n_seq, seq, N_HEADS, HEAD_DIM))
            k_s.append(kfs.reshape(n_dec, 1, N_HEADS, HEAD_DIM))
            v_s.append(vfs.reshape(n_dec, 1, N_HEADS, HEAD_DIM))
            mix = (w_o[j].astype(BF16), zero_bias, row(norm_ffn[i]))
            wr = jnp.zeros((D_MODEL, LANES), F32).at[:, :N_EXPERTS].set(w_router[j])
            moe = (wr, w_moe_gate[j].astype(BF16), w_moe_up[j].astype(BF16), w_moe_down[j].astype(BF16))
            hp = _mixer_moe(hp, xp, *mix, *moe, pp, *ple, final, PROMPT_ROWS)
            hs = _mixer_moe(hs, xs, *mix, *moe, ps, *ple, final, n_dec)
    return (hp.reshape(n_seq, seq, D_MODEL), hs.reshape(n_dec, 1, D_MODEL),
            jnp.stack(k_p), jnp.stack(v_p), jnp.stack(k_s), jnp.stack(v_s),
            jnp.stack(conv_p), jnp.stack(conv_s))
```

```python
import functools

import jax
import jax.numpy as jnp
from jax import lax
from jax.experimental import pallas as pl
from jax.experimental.pallas import tpu as pltpu

D_MODEL = 1024
N_HEADS = 16
HEAD_DIM = 64
SB_SCALE = HEAD_DIM ** -0.5
CONV_WIDTH = 31
D_FF = 2816
N_EXPERTS = 8
D_PLE = 256
RMS_EPS = 1e-6
LN_EPS = 1e-5
PAGE_SIZE = 128

LANES = 128
HALO_ROWS = 32
PROMPT_ROWS = 512
CONV_ROWS = 256
ATTN_ROWS = 256
FF_CHUNK = 256
MOE_CHUNKS = 2
DEC_PAGES = 8
EXPERT_ROWS = 512
CAST_ROWS = 256
VMEM_LIMIT = 56 * 1024 * 1024
LOG2E = 1.4426950408889634
MASKED = -1e30
META_E1, META_E2, META_R1, META_R2, META_W1, META_W2 = range(6)

BF16 = jnp.bfloat16
F32 = jnp.float32


def _params(n_axes):
    return pltpu.CompilerParams(dimension_semantics=("arbitrary",) * n_axes,
                                vmem_limit_bytes=VMEM_LIMIT)


def _rms(x, g):
    ms = jnp.mean(x * x, axis=-1, keepdims=True)
    return x * lax.rsqrt(ms + RMS_EPS) * g


def _sigmoid(x):
    return 1.0 / (1.0 + jnp.exp(-x))


def _const_spec(shape):
    zeros = (0,) * len(shape)
    return pl.BlockSpec(shape, lambda *_: zeros)


def _cast_kernel(x_ref, o_ref):
    o_ref[...] = x_ref[...].astype(BF16)


def _to_bf16(w):
    rows, cols = w.shape[-2:]
    w3 = w.reshape(-1, rows, cols)
    spec = pl.BlockSpec((None, CAST_ROWS, cols), lambda a, r: (a, r, 0))
    out = pl.pallas_call(
        _cast_kernel,
        grid=(w3.shape[0], rows // CAST_ROWS),
        in_specs=[spec],
        out_specs=spec,
        out_shape=jax.ShapeDtypeStruct(w3.shape, BF16),
        compiler_params=_params(2),
        name="cast_bf16",
    )(w3)
    return out.reshape(w.shape)


def _glu_kernel(h_ref, g_ref, w_ref, b_ref, o_ref):
    u = _rms(h_ref[...], g_ref[...]).astype(BF16)
    a = jnp.dot(u, w_ref[...], preferred_element_type=F32) + b_ref[...]
    o_ref[...] = a[:, :D_MODEL] * _sigmoid(a[:, D_MODEL:])


def _glu(h, g, w, b, tm):
    m = h.shape[0]
    return pl.pallas_call(
        _glu_kernel,
        grid=(m // tm,),
        in_specs=[pl.BlockSpec((tm, D_MODEL), lambda i: (i, 0)),
                  _const_spec((1, D_MODEL)),
                  _const_spec((D_MODEL, 2 * D_MODEL)),
                  _const_spec((1, 2 * D_MODEL))],
        out_specs=pl.BlockSpec((tm, D_MODEL), lambda i: (i, 0)),
        out_shape=jax.ShapeDtypeStruct((m, D_MODEL), F32),
        compiler_params=_params(1),
        name="glu",
    )(h, g, w, b)


def _ln_swish(y, lg, lb):
    mu = jnp.mean(y, axis=-1, keepdims=True)
    yc = y - mu
    var = jnp.mean(yc * yc, axis=-1, keepdims=True)
    yn = yc * lax.rsqrt(var + LN_EPS) * lg + lb
    return yn * _sigmoid(yn)


def _dwconv_kernel(halo_ref, x_ref, wd_ref, bd_ref, lg_ref, lb_ref, o_ref, buf, sh, ybuf):
    tq = CONV_ROWS
    i = pl.program_id(1)
    buf[0:HALO_ROWS, :] = jnp.where(i > 0, halo_ref[...], 0.0)
    buf[HALO_ROWS:HALO_ROWS + tq, :] = x_ref[...]
    base = HALO_ROWS - (CONV_WIDTH - 1)
    n_taps = [len(range(r, CONV_WIDTH, 8)) for r in range(8)]
    for r in range(8):
        n = tq + 8 * (n_taps[r] - 1)
        sh[r, 0:n, :] = buf[base + r:base + r + n, :]

    rb = 64

    def col_body(c, carry):
        cs = pl.ds(pl.multiple_of(c * LANES, LANES), LANES)
        for blk in range(tq // rb):
            acc = jnp.zeros((rb, LANES), F32) + bd_ref[:, cs]
            for r in range(8):
                for a in range(n_taps[r]):
                    w = 8 * a + r
                    lo = 8 * a + blk * rb
                    acc = acc + sh[r, lo:lo + rb, cs] * wd_ref[w:w + 1, cs]
            ybuf[blk * rb:(blk + 1) * rb, cs] = acc
        return carry

    lax.fori_loop(0, D_MODEL // LANES, col_body, 0)
    o_ref[...] = _ln_swish(ybuf[...], lg_ref[...], lb_ref[...]).astype(BF16)


def _dwconv_prompt(glu, n_seq, seq, wd, bd, lg, lb):
    tq = CONV_ROWS
    nq = seq // tq
    halo_per_blk = tq // HALO_ROWS

    def halo_map(b, i):
        return (jnp.maximum(b * (seq // HALO_ROWS) + i * halo_per_blk - 1, 0), 0)

    return pl.pallas_call(
        _dwconv_kernel,
        grid=(n_seq, nq),
        in_specs=[pl.BlockSpec((HALO_ROWS, D_MODEL), halo_map),
                  pl.BlockSpec((tq, D_MODEL), lambda b, i: (b * nq + i, 0)),
                  _const_spec((CONV_WIDTH, D_MODEL)),
                  _const_spec((1, D_MODEL)),
                  _const_spec((1, D_MODEL)),
                  _const_spec((1, D_MODEL))],
        out_specs=pl.BlockSpec((tq, D_MODEL), lambda b, i: (b * nq + i, 0)),
        out_shape=jax.ShapeDtypeStruct((n_seq * seq, D_MODEL), BF16),
        scratch_shapes=[pltpu.VMEM((HALO_ROWS + tq, D_MODEL), F32),
                        pltpu.VMEM((8, tq + 24, D_MODEL), F32),
                        pltpu.VMEM((tq, D_MODEL), F32)],
        compiler_params=_params(2),
        name="dwconv_prompt",
    )(glu, glu, wd, bd, lg, lb)


def _dwconv_sample_kernel(hist_ref, x_ref, wd_ref, bd_ref, lg_ref, lb_ref, o_ref):
    acc = x_ref[...] * wd_ref[CONV_WIDTH - 1:CONV_WIDTH, :] + bd_ref[...]
    for w in range(CONV_WIDTH - 1):
        acc = acc + hist_ref[w] * wd_ref[w:w + 1, :]
    o_ref[...] = _ln_swish(acc, lg_ref[...], lb_ref[...]).astype(BF16)


def _dwconv_sample(hist_t, glu, wd, bd, lg, lb):
    n = glu.shape[0]
    return pl.pallas_call(
        _dwconv_sample_kernel,
        grid=(1,),
        in_specs=[_const_spec((CONV_WIDTH - 1, n, D_MODEL)),
                  _const_spec((n, D_MODEL)),
                  _const_spec((CONV_WIDTH, D_MODEL)),
                  _const_spec((1, D_MODEL)),
                  _const_spec((1, D_MODEL)),
                  _const_spec((1, D_MODEL))],
        out_specs=_const_spec((n, D_MODEL)),
        out_shape=jax.ShapeDtypeStruct((n, D_MODEL), BF16),
        compiler_params=_params(1),
        name="dwconv_sample",
    )(hist_t, glu, wd, bd, lg, lb)


def _qkv_kernel(h_ref, g_ref, w_ref, kf_ref, vf_ref, qb_ref, kb_ref, vb_ref):
    u = _rms(h_ref[...], g_ref[...]).astype(BF16)
    qkv = jnp.dot(u, w_ref[...], preferred_element_type=F32)
    k = qkv[:, D_MODEL:2 * D_MODEL]
    v = qkv[:, 2 * D_MODEL:]
    kf_ref[...] = k
    vf_ref[...] = v
    qb_ref[...] = (qkv[:, :D_MODEL] * (SB_SCALE * LOG2E)).astype(BF16)
    kb_ref[...] = k.astype(BF16)
    vb_ref[...] = v.astype(BF16)


def _qkv(h, g, w, tm):
    m = h.shape[0]
    row = pl.BlockSpec((tm, D_MODEL), lambda i: (i, 0))
    return pl.pallas_call(
        _qkv_kernel,
        grid=(m // tm,),
        in_specs=[row, _const_spec((1, D_MODEL)), _const_spec((D_MODEL, 3 * D_MODEL))],
        out_specs=[row] * 5,
        out_shape=[jax.ShapeDtypeStruct((m, D_MODEL), F32)] * 2
        + [jax.ShapeDtypeStruct((m, D_MODEL), BF16)] * 3,
        compiler_params=_params(1),
        name="qkv",
    )(h, g, w)


def _softplus2(y):
    neg_abs = lax.bitcast_convert_type(
        lax.bitcast_convert_type(y, jnp.uint32) | jnp.uint32(0x80000000), F32)
    return jnp.maximum(y, 0.0) + jnp.log(1.0 + jnp.exp2(neg_abs)) * LOG2E


def _attn_kernel(bias_ref, q_ref, k_ref, v_ref, t_ref, o_ref, *scratch):
    t = ATTN_ROWS
    nq = q_ref.shape[0] // t
    n_tiles = nq * (nq + 1) // 2
    hp = pl.program_id(1)
    heads = (scratch[:7], scratch[7:])
    lane = lax.broadcasted_iota(jnp.int32, (1, LANES), 1)
    first = lane < HEAD_DIM
    q = q_ref[...]
    row = lax.broadcasted_iota(jnp.int32, (t, t), 0)
    col = lax.broadcasted_iota(jnp.int32, (t, t), 1)
    for hh, (qm, bm, o_acc, c_acc, y_buf, sp_buf, a_buf) in enumerate(heads):
        keep = first if hh == 0 else jnp.logical_not(first)
        qm[...] = jnp.where(keep, q, jnp.zeros_like(q))
        bias = jnp.full((t, t), bias_ref[2 * hp + hh], F32)
        bm[0] = bias
        bm[1] = jnp.where(col < row, bias, MASKED)
        bm[2] = jnp.full((t, t), MASKED, F32)
        o_acc[...] = jnp.zeros_like(o_acc)
        c_acc[...] = jnp.zeros_like(c_acc)
        y_buf[...] = jnp.full(y_buf.shape, MASKED, F32)
        sp_buf[...] = jnp.zeros_like(sp_buf)
        a_buf[...] = jnp.zeros_like(a_buf)

    def rows(blk):
        return pl.ds(pl.multiple_of(blk * t, t), t)

    def body(f, tiles):
        (i0, j0), _, (i2, _), (i3, j3) = tiles
        kind = jnp.where(f >= n_tiles, 2, jnp.where(j0 == i0, 1, 0))
        y0, y1, y2 = lax.rem(f, 3), lax.rem(f + 2, 3), lax.rem(f + 1, 3)
        even, odd = lax.rem(f, 2), lax.rem(f + 1, 2)
        kblk = k_ref[rows(j0), :]
        vblk = v_ref[rows(j3), :]
        for qm, bm, o_acc, c_acc, y_buf, sp_buf, a_buf in heads:
            o_acc[rows(i3), :] += jnp.dot(a_buf[odd], vblk, preferred_element_type=F32)
            cum = jnp.dot(sp_buf[even], t_ref[...], preferred_element_type=F32)
            c = c_acc[i2]
            a_buf[even] = jnp.exp2(y_buf[y2] - cum - c).astype(BF16)
            c_acc[i2] = c + cum[:, 0:1]
            sp_buf[odd] = _softplus2(y_buf[y1]).astype(BF16)
            y_buf[y0] = lax.dot_general(
                qm[rows(i0), :], kblk, (((1,), (1,)), ((), ())),
                preferred_element_type=F32) + bm[kind]
        wrap = j0 == 0
        i_next = jnp.minimum(jnp.where(wrap, i0 + 1, i0), nq - 1)
        j_next = jnp.minimum(jnp.where(wrap, i0 + 1, j0 - 1), nq - 1)
        return ((i_next, j_next),) + tiles[:3]

    zero = jnp.int32(0)
    lax.fori_loop(0, n_tiles + 3, body, ((zero, zero),) * 4)
    o_ref[...] = jnp.where(first, heads[0][2][...], heads[1][2][...]).astype(BF16)


def _attn_prompt(sb_bias, qb, kb, vb, n_seq, seq):
    t = ATTN_ROWS
    nq = seq // t
    tmat = (jnp.arange(t)[:, None] >= jnp.arange(t)[None, :]).astype(BF16)
    seq_spec = pl.BlockSpec((None, seq, LANES), lambda b, hp: (b, 0, hp))
    per_head = [pltpu.VMEM((seq, LANES), BF16),
                pltpu.VMEM((3, t, t), F32),
                pltpu.VMEM((seq, LANES), F32),
                pltpu.VMEM((nq, t, 1), F32),
                pltpu.VMEM((3, t, t), F32),
                pltpu.VMEM((2, t, t), BF16),
                pltpu.VMEM((2, t, t), BF16)]
    out = pl.pallas_call(
        _attn_kernel,
        grid=(n_seq, N_HEADS // 2),
        in_specs=[pl.BlockSpec(memory_space=pltpu.SMEM),
                  seq_spec, seq_spec, seq_spec,
                  _const_spec((t, t))],
        out_specs=seq_spec,
        out_shape=jax.ShapeDtypeStruct((n_seq, seq, D_MODEL), BF16),
        scratch_shapes=per_head * 2,
        compiler_params=_params(2),
        name="attn_prompt",
    )(sb_bias * LOG2E, qb.reshape(n_seq, seq, D_MODEL), kb.reshape(n_seq, seq, D_MODEL),
      vb.reshape(n_seq, seq, D_MODEL), tmat)
    return out.reshape(n_seq * seq, D_MODEL)


def _decode_kernel(pt_ref, qm_ref, bias_ref, tl_ref, ex_ref, *refs):
    k_refs = refs[:DEC_PAGES]
    v_refs = refs[DEC_PAGES:2 * DEC_PAGES]
    o_ref, acc, carry = refs[2 * DEC_PAGES:]
    g = pl.program_id(1)

    @pl.when(g == 0)
    def _():
        acc[...] = jnp.zeros_like(acc)
        carry[...] = jnp.zeros_like(carry)

    qm = qm_ref[...]
    for s in range(DEC_PAGES):
        y = jnp.dot(k_refs[s][...].astype(BF16), qm, preferred_element_type=F32) + bias_ref[...]
        sp = _softplus2(y)
        cum = jnp.dot(tl_ref[...], sp.astype(BF16), preferred_element_type=F32)
        a = jnp.exp2(y - cum - carry[...])
        carry[...] += cum[0:1, :]
        a_wide = jnp.dot(a.astype(BF16), ex_ref[...], preferred_element_type=F32)
        prod = a_wide * v_refs[s][...]
        acc[...] += jnp.sum(prod.reshape(PAGE_SIZE // 8, 8, D_MODEL), axis=0)

    @pl.when(g == pl.num_programs(1) - 1)
    def _():
        o_ref[...] = jnp.sum(acc[...], axis=0, keepdims=True)


def _attn_decode(page_table, q_scaled, sb_bias, cache_k, cache_v):
    n, n_pages = page_table.shape
    head_of_lane = jnp.arange(D_MODEL) // HEAD_DIM
    sel = head_of_lane[:, None] == jnp.arange(LANES)[None, :]
    qm = jnp.where(sel[None], q_scaled[:, :, None], 0.0).astype(BF16)
    bias = jnp.zeros((1, LANES), F32).at[0, :N_HEADS].set(sb_bias * LOG2E)
    tl = (jnp.arange(PAGE_SIZE)[None, :] >= jnp.arange(PAGE_SIZE)[:, None]).astype(BF16)
    ex = sel.T.astype(BF16)
    steps = n_pages // DEC_PAGES

    def page_spec(s):
        def index_map(b, g, pt):
            return (pt[b, n_pages - 1 - g * DEC_PAGES - s], 0, 0)
        return pl.BlockSpec((None, PAGE_SIZE, D_MODEL), index_map)

    grid_spec = pltpu.PrefetchScalarGridSpec(
        num_scalar_prefetch=1,
        grid=(n, steps),
        in_specs=[pl.BlockSpec((None, D_MODEL, LANES), lambda b, g, pt: (b, 0, 0)),
                  pl.BlockSpec((1, LANES), lambda b, g, pt: (0, 0)),
                  pl.BlockSpec((PAGE_SIZE, PAGE_SIZE), lambda b, g, pt: (0, 0)),
                  pl.BlockSpec((LANES, D_MODEL), lambda b, g, pt: (0, 0))]
        + [page_spec(s) for s in range(DEC_PAGES)] * 2,
        out_specs=pl.BlockSpec((None, 1, D_MODEL), lambda b, g, pt: (b, 0, 0)),
        scratch_shapes=[pltpu.VMEM((8, D_MODEL), F32), pltpu.VMEM((1, LANES), F32)],
    )
    out = pl.pallas_call(
        _decode_kernel,
        grid_spec=grid_spec,
        out_shape=jax.ShapeDtypeStruct((n, 1, D_MODEL), F32),
        compiler_params=_params(2),
        name="attn_decode",
    )(page_table, qm, bias, tl, ex, *([cache_k] * DEC_PAGES), *([cache_v] * DEC_PAGES))
    return out.reshape(n, D_MODEL)


def _ple_tail(hres, p_ref, gple_ref, wgate_ref, wproj_ref, gfin_ref, final):
    r = _rms(hres, gple_ref[...]).astype(BF16)
    gate = _sigmoid(jnp.dot(r, wgate_ref[...], preferred_element_type=F32))
    proj = jnp.dot(p_ref[...].astype(BF16), wproj_ref[...], preferred_element_type=F32)
    out = hres + gate * proj
    if final:
        out = _rms(out, gfin_ref[...])
    return out


def _mix_in(h_ref, x_ref, wmix_ref, bmix_ref):
    return h_ref[...] + jnp.dot(x_ref[...].astype(BF16), wmix_ref[...],
                                preferred_element_type=F32) + bmix_ref[...]


def _silu_mul(g, u):
    return (g * _sigmoid(g) * u).astype(BF16)


def _mixer_dense_kernel(h_ref, x_ref, wmix_ref, bmix_ref, gffn_ref, wg_ref, wu_ref, wd_ref,
                        p_ref, gple_ref, wgate_ref, wproj_ref, gfin_ref, o_ref, *, final):
    h1 = _mix_in(h_ref, x_ref, wmix_ref, bmix_ref)
    f = _rms(h1, gffn_ref[...]).astype(BF16)
    acc = h1
    for c in range(D_FF // FF_CHUNK):
        cs = slice(c * FF_CHUNK, (c + 1) * FF_CHUNK)
        g = jnp.dot(f, wg_ref[:, cs], preferred_element_type=F32)
        u = jnp.dot(f, wu_ref[:, cs], preferred_element_type=F32)
        acc = acc + jnp.dot(_silu_mul(g, u), wd_ref[cs, :], preferred_element_type=F32)
    o_ref[...] = _ple_tail(acc, p_ref, gple_ref, wgate_ref, wproj_ref, gfin_ref, final)


def _single(shape):
    zeros = (0,) * len(shape)
    return pl.BlockSpec(shape, lambda *_: zeros, pipeline_mode=pl.Buffered(1))


def _mixer_dense(h, x, wmix, bmix, gffn, wg, wu, wd, p, gple, wgate, wproj, gfin, final, tm):
    m = h.shape[0]
    row = lambda width: pl.BlockSpec((tm, width), lambda i: (i, 0))
    return pl.pallas_call(
        functools.partial(_mixer_dense_kernel, final=final),
        grid=(m // tm,),
        in_specs=[row(D_MODEL), row(D_MODEL),
                  _single((D_MODEL, D_MODEL)), _single((1, D_MODEL)), _single((1, D_MODEL)),
                  _single((D_MODEL, D_FF)), _single((D_MODEL, D_FF)), _single((D_FF, D_MODEL)),
                  row(D_PLE), _single((1, D_MODEL)),
                  _single((D_MODEL, D_MODEL)), _single((D_PLE, D_MODEL)), _single((1, D_MODEL))],
        out_specs=row(D_MODEL),
        out_shape=jax.ShapeDtypeStruct((m, D_MODEL), F32),
        compiler_params=_params(1),
        name="mixer_dense",
    )(h, x, wmix, bmix, gffn, wg, wu, wd, p, gple, wgate, wproj, gfin)


def _top2(logits):
    lane = lax.broadcasted_iota(jnp.int32, logits.shape, 1).astype(F32)
    neg = -jnp.inf
    lg = jnp.where(lane < N_EXPERTS, logits, neg)
    m1 = jnp.max(lg, axis=1, keepdims=True)
    i1 = jnp.min(jnp.where(lg == m1, lane, float(LANES)), axis=1, keepdims=True)
    lg2 = jnp.where(lane == i1, neg, lg)
    m2 = jnp.max(lg2, axis=1, keepdims=True)
    i2 = jnp.min(jnp.where(lg2 == m2, lane, float(LANES)), axis=1, keepdims=True)
    t = jnp.exp(m2 - m1)
    return lane, i1, i2, 1.0 / (1.0 + t), t / (1.0 + t)


def _router_logits(f, wr_ref):
    return jnp.dot(f, wr_ref[...], preferred_element_type=F32, precision=lax.Precision.HIGHEST)


def _mixer_moe_kernel(h_ref, x_ref, wmix_ref, bmix_ref, gffn_ref, wr_ref, wg_ref, wu_ref, wd_ref,
                      p_ref, gple_ref, wgate_ref, wproj_ref, gfin_ref, o_ref,
                      f_scr, gate_scr, acc_scr, *, final):
    e = pl.program_id(1)
    c = pl.program_id(2)

    @pl.when((e == 0) & (c == 0))
    def _():
        h1 = _mix_in(h_ref, x_ref, wmix_ref, bmix_ref)
        f = _rms(h1, gffn_ref[...])
        lane, i1, i2, w1, w2 = _top2(_router_logits(f, wr_ref))
        gate_scr[...] = jnp.where(lane == i1, w1, 0.0) + jnp.where(lane == i2, w2, 0.0)
        f_scr[...] = f.astype(BF16)
        acc_scr[...] = h1

    f = f_scr[...]
    g = jnp.dot(f, wg_ref[...], preferred_element_type=F32)
    u = jnp.dot(f, wu_ref[...], preferred_element_type=F32)
    y = jnp.dot(_silu_mul(g, u), wd_ref[...], preferred_element_type=F32)
    gate = gate_scr[...]
    lane = lax.broadcasted_iota(jnp.int32, gate.shape, 1)
    ge = jnp.sum(jnp.where(lane == e, gate, 0.0), axis=1, keepdims=True)
    acc_scr[...] += ge * y

    @pl.when((e == pl.num_programs(1) - 1) & (c == pl.num_programs(2) - 1))
    def _():
        o_ref[...] = _ple_tail(acc_scr[...], p_ref, gple_ref, wgate_ref, wproj_ref, gfin_ref, final)


def _mixer_moe(h, x, wmix, bmix, gffn, wr, wg, wu, wd, p, gple, wgate, wproj, gfin, final, tm):
    m = h.shape[0]
    fc = D_FF // MOE_CHUNKS
    row = lambda width: pl.BlockSpec((tm, width), lambda i, e, c: (i, 0))
    return pl.pallas_call(
        functools.partial(_mixer_moe_kernel, final=final),
        grid=(m // tm, N_EXPERTS, MOE_CHUNKS),
        in_specs=[row(D_MODEL), row(D_MODEL),
                  _single((D_MODEL, D_MODEL)), _single((1, D_MODEL)), _single((1, D_MODEL)),
                  _single((D_MODEL, LANES)),
                  pl.BlockSpec((None, D_MODEL, fc), lambda i, e, c: (e, 0, c)),
                  pl.BlockSpec((None, D_MODEL, fc), lambda i, e, c: (e, 0, c)),
                  pl.BlockSpec((None, fc, D_MODEL), lambda i, e, c: (e, c, 0)),
                  row(D_PLE), _single((1, D_MODEL)),
                  _single((D_MODEL, D_MODEL)), _single((D_PLE, D_MODEL)), _single((1, D_MODEL))],
        out_specs=row(D_MODEL),
        out_shape=jax.ShapeDtypeStruct((m, D_MODEL), F32),
        scratch_shapes=[pltpu.VMEM((tm, D_MODEL), BF16),
                        pltpu.VMEM((tm, LANES), F32),
                        pltpu.VMEM((tm, D_MODEL), F32)],
        compiler_params=_params(3),
        name="mixer_moe",
    )(h, x, wmix, bmix, gffn, wr, wg, wu, wd, p, gple, wgate, wproj, gfin)


def _moe_route_kernel(h_ref, x_ref, wmix_ref, bmix_ref, gffn_ref, wr_ref, ltri_ref,
                      h1_ref, f_ref, meta_ref, cnt_ref):
    h1 = _mix_in(h_ref, x_ref, wmix_ref, bmix_ref)
    f = _rms(h1, gffn_ref[...])
    lane, i1, i2, w1, w2 = _top2(_router_logits(f, wr_ref))
    sel = jnp.where(lane == i1, 1.0, 0.0) + jnp.where(lane == i2, 1.0, 0.0)
    rank = jnp.dot(ltri_ref[...], sel.astype(BF16), preferred_element_type=F32)
    r1 = jnp.sum(jnp.where(lane == i1, rank, 0.0), axis=1, keepdims=True)
    r2 = jnp.sum(jnp.where(lane == i2, rank, 0.0), axis=1, keepdims=True)
    meta = jnp.zeros_like(rank)
    for pos, val in ((META_E1, i1), (META_E2, i2), (META_R1, r1), (META_R2, r2),
                     (META_W1, w1), (META_W2, w2)):
        meta = jnp.where(lane == float(pos), val, meta)
    h1_ref[...] = h1
    f_ref[...] = f
    meta_ref[...] = meta
    cnt_ref[...] = jnp.sum(sel, axis=0, keepdims=True)


def _moe_route(h, x, wmix, bmix, gffn, wr, tm):
    m = h.shape[0]
    ltri = (jnp.arange(tm)[:, None] > jnp.arange(tm)[None, :]).astype(BF16)
    row = lambda width: pl.BlockSpec((tm, width), lambda i: (i, 0))
    return pl.pallas_call(
        _moe_route_kernel,
        grid=(m // tm,),
        in_specs=[row(D_MODEL), row(D_MODEL),
                  _single((D_MODEL, D_MODEL)), _single((1, D_MODEL)), _single((1, D_MODEL)),
                  _single((D_MODEL, LANES)), _single((tm, tm))],
        out_specs=[row(D_MODEL), row(D_MODEL), row(LANES),
                   pl.BlockSpec((None, 1, LANES), lambda i: (i, 0, 0))],
        out_shape=[jax.ShapeDtypeStruct((m, D_MODEL), F32), jax.ShapeDtypeStruct((m, D_MODEL), F32),
                   jax.ShapeDtypeStruct((m, LANES), F32),
                   jax.ShapeDtypeStruct((m // tm, 1, LANES), F32)],
        compiler_params=_params(1),
        name="moe_route",
    )(h, x, wmix, bmix, gffn, wr, ltri)


def _route_plan(meta, cnt, tm):
    m = meta.shape[0]
    n_rows = 2 * m + N_EXPERTS * EXPERT_ROWS
    expert = meta[:, META_E1:META_E2 + 1].astype(jnp.int32)
    rank = meta[:, META_R1:META_R2 + 1].astype(jnp.int32)
    cnt = cnt[:, 0, :N_EXPERTS].astype(jnp.int32)
    total = jnp.sum(cnt, axis=0)
    region = (total + EXPERT_ROWS - 1) // EXPERT_ROWS * EXPERT_ROWS
    region_end = jnp.cumsum(region)
    base = (region_end - region)[None, :] + jnp.cumsum(cnt, axis=0) - cnt
    block = (jnp.arange(m) // tm)[:, None]
    dest = (base[block, expert] + rank).reshape(-1)
    starts = jnp.arange(n_rows // EXPERT_ROWS) * EXPERT_ROWS
    tile_expert = jnp.minimum(jnp.sum(starts[:, None] >= region_end[None, :], axis=1),
                              N_EXPERTS - 1).astype(jnp.int32)
    active = (starts < region_end[-1]).astype(jnp.int32)
    return dest.astype(jnp.int32), tile_expert, active, n_rows


def _row_copies(idx_ref, n, make):
    def start(m, carry):
        for k in range(2):
            make(m, k, idx_ref[2 * m + k]).start()
        return carry

    def wait(m, carry):
        for k in range(2):
            make(m, k, idx_ref[2 * m + k]).wait()
        return carry

    lax.fori_loop(0, n, start, 0)
    lax.fori_loop(0, n, wait, 0)


def _moe_scatter_kernel(dest_ref, f_ref, zeros_ref, xs_ref, sem):
    del zeros_ref

    def make(m, k, d):
        return pltpu.make_async_copy(f_ref.at[pl.ds(m, 1)], xs_ref.at[pl.ds(d, 1)], sem)

    _row_copies(dest_ref, f_ref.shape[0], make)


def _moe_scatter(dest, f, n_rows, tm):
    m = f.shape[0]
    return pl.pallas_call(
        _moe_scatter_kernel,
        grid=(m // tm,),
        in_specs=[pl.BlockSpec((2 * tm,), lambda i: (i,), memory_space=pltpu.SMEM),
                  pl.BlockSpec((tm, D_MODEL), lambda i: (i, 0)),
                  pl.BlockSpec(memory_space=pl.ANY)],
        out_specs=pl.BlockSpec(memory_space=pl.ANY),
        out_shape=jax.ShapeDtypeStruct((n_rows, D_MODEL), F32),
        scratch_shapes=[pltpu.SemaphoreType.DMA(())],
        input_output_aliases={2: 0},
        compiler_params=_params(1),
        name="moe_scatter",
    )(dest, f, jnp.zeros((n_rows, D_MODEL), F32))


def _moe_expert_kernel(expert_ref, active_ref, x_ref, wg_ref, wu_ref, wd_ref, o_ref):
    del expert_ref
    k = pl.program_id(0)

    @pl.when(active_ref[k] != 0)
    def _():
        x = x_ref[...].astype(BF16)
        acc = None
        for c in range(D_FF // FF_CHUNK):
            cs = slice(c * FF_CHUNK, (c + 1) * FF_CHUNK)
            g = jnp.dot(x, wg_ref[:, cs], preferred_element_type=F32)
            u = jnp.dot(x, wu_ref[:, cs], preferred_element_type=F32)
            y = jnp.dot(_silu_mul(g, u), wd_ref[cs, :], preferred_element_type=F32)
            acc = y if acc is None else acc + y
        o_ref[...] = acc

    @pl.when(active_ref[k] == 0)
    def _():
        o_ref[...] = jnp.zeros_like(o_ref)


def _moe_experts(tile_expert, active, xs, wg, wu, wd):
    n_rows = xs.shape[0]
    rows = pl.BlockSpec((EXPERT_ROWS, D_MODEL), lambda k, te, ac: (k, 0))
    grid_spec = pltpu.PrefetchScalarGridSpec(
        num_scalar_prefetch=2,
        grid=(n_rows // EXPERT_ROWS,),
        in_specs=[rows,
                  pl.BlockSpec((None, D_MODEL, D_FF), lambda k, te, ac: (te[k], 0, 0)),
                  pl.BlockSpec((None, D_MODEL, D_FF), lambda k, te, ac: (te[k], 0, 0)),
                  pl.BlockSpec((None, D_FF, D_MODEL), lambda k, te, ac: (te[k], 0, 0))],
        out_specs=rows,
    )
    return pl.pallas_call(
        _moe_expert_kernel,
        grid_spec=grid_spec,
        out_shape=jax.ShapeDtypeStruct((n_rows, D_MODEL), F32),
        compiler_params=_params(1),
        name="moe_experts",
    )(tile_expert, active, xs, wg, wu, wd)


def _moe_combine_kernel(dest_ref, ys_ref, h1_ref, meta_ref, p_ref, gple_ref, wgate_ref, wproj_ref,
                        gfin_ref, o_ref, ybuf, sem, *, final):
    def make(m, k, d):
        return pltpu.make_async_copy(ys_ref.at[pl.ds(d, 1)], ybuf.at[k, pl.ds(m, 1)], sem)

    _row_copies(dest_ref, h1_ref.shape[0], make)
    meta = meta_ref[...]
    lane = lax.broadcasted_iota(jnp.int32, meta.shape, 1)
    w1 = jnp.sum(jnp.where(lane == META_W1, meta, 0.0), axis=1, keepdims=True)
    w2 = jnp.sum(jnp.where(lane == META_W2, meta, 0.0), axis=1, keepdims=True)
    hres = h1_ref[...] + (w1 * ybuf[0] + w2 * ybuf[1])
    o_ref[...] = _ple_tail(hres, p_ref, gple_ref, wgate_ref, wproj_ref, gfin_ref, final)


def _moe_combine(dest, ys, h1, meta, p, gple, wgate, wproj, gfin, final, tm):
    m = h1.shape[0]
    row = lambda width: pl.BlockSpec((tm, width), lambda i: (i, 0))
    return pl.pallas_call(
        functools.partial(_moe_combine_kernel, final=final),
        grid=(m // tm,),
        in_specs=[pl.BlockSpec((2 * tm,), lambda i: (i,), memory_space=pltpu.SMEM),
                  pl.BlockSpec(memory_space=pl.ANY),
                  row(D_MODEL), row(LANES), row(D_PLE), _single((1, D_MODEL)),
                  _single((D_MODEL, D_MODEL)), _single((D_PLE, D_MODEL)), _single((1, D_MODEL))],
        out_specs=row(D_MODEL),
        out_shape=jax.ShapeDtypeStruct((m, D_MODEL), F32),
        scratch_shapes=[pltpu.VMEM((2, tm, D_MODEL), F32), pltpu.SemaphoreType.DMA(())],
        compiler_params=_params(1),
        name="moe_combine",
    )(dest, ys, h1, meta, p, gple, wgate, wproj, gfin)


def _mixer_moe_routed(h, x, wmix, bmix, gffn, wr, wg, wu, wd, p, gple, wgate, wproj, gfin, final, tm):
    h1, f, meta, cnt = _moe_route(h, x, wmix, bmix, gffn, wr, tm)
    dest, tile_expert, active, n_rows = _route_plan(meta, cnt, tm)
    xs = _moe_scatter(dest, f, n_rows, tm)
    ys = _moe_experts(tile_expert, active, xs, wg, wu, wd)
    return _moe_combine(dest, ys, h1, meta, p, gple, wgate, wproj, gfin, final, tm)


def kernel(x_prompt, x_sample, cache_k, cache_v, state_conv, page_table, p_prompt, p_sample, norm_mix, w_pw1, b_pw1, w_dw, b_dw, ln_conv_g, ln_conv_b, w_pw2, b_pw2, w_qkv, w_o, sb_bias, norm_ffn, w_ffn_gate, w_ffn_up, w_ffn_down, w_router, w_moe_gate, w_moe_up, w_moe_down, norm_ple, w_ple_gate, w_ple_proj, norm_final):
    n_seq, seq, _ = x_prompt.shape
    n_dec = x_sample.shape[0]
    depth = norm_mix.shape[0]
    mp = n_seq * seq
    hp = x_prompt.reshape(mp, D_MODEL)
    hs = x_sample.reshape(n_dec, D_MODEL)
    row = lambda a: a.reshape(1, -1)
    zero_bias = jnp.zeros((1, D_MODEL), F32)
    gfin = row(norm_final)
    w_pw1, w_pw2, w_qkv, w_o = map(_to_bf16, (w_pw1, w_pw2, w_qkv, w_o))
    w_ffn_gate, w_ffn_up, w_ffn_down = map(_to_bf16, (w_ffn_gate, w_ffn_up, w_ffn_down))
    w_moe_gate, w_moe_up, w_moe_down = map(_to_bf16, (w_moe_gate, w_moe_up, w_moe_down))
    w_ple_gate, w_ple_proj = map(_to_bf16, (w_ple_gate, w_ple_proj))
    last_rows = CONV_WIDTH - 1

    k_p, v_p, k_s, v_s, conv_p, conv_s = [], [], [], [], [], []
    for i in range(depth):
        j = i // 2
        final = i == depth - 1
        gmix = row(norm_mix[i])
        ple = (row(norm_ple[i]), w_ple_gate[i], w_ple_proj[i], gfin)
        pp = p_prompt[i].reshape(mp, D_PLE)
        ps = p_sample[i].reshape(n_dec, D_PLE)
        if i % 2 == 0:
            w1 = w_pw1[j]
            b1 = row(b_pw1[j])
            conv_w = (w_dw[j], row(b_dw[j]), row(ln_conv_g[j]), row(ln_conv_b[j]))
            glu_p = _glu(hp, gmix, w1, b1, PROMPT_ROWS)
            glu_s = _glu(hs, gmix, w1, b1, n_dec)
            xp = _dwconv_prompt(glu_p, n_seq, seq, *conv_w)
            xs = _dwconv_sample(jnp.swapaxes(state_conv[j], 0, 1), glu_s, *conv_w)
            conv_p.append(glu_p.reshape(n_seq, seq, D_MODEL)[:, seq - last_rows:])
            conv_s.append(jnp.concatenate([state_conv[j][:, 1:], glu_s[:, None, :]], axis=1))
            mix = (w_pw2[j], row(b_pw2[j]), row(norm_ffn[i]))
            ffn = (w_ffn_gate[j], w_ffn_up[j], w_ffn_down[j])
            hp = _mixer_dense(hp, xp, *mix, *ffn, pp, *ple, final, PROMPT_ROWS)
            hs = _mixer_dense(hs, xs, *mix, *ffn, ps, *ple, final, n_dec)
        else:
            wq = w_qkv[j]
            kf, vf, qb, kb, vb = _qkv(hp, gmix, wq, PROMPT_ROWS)
            xp = _attn_prompt(sb_bias[j], qb, kb, vb, n_seq, seq)
            kfs, vfs, qbs, _, _ = _qkv(hs, gmix, wq, n_dec)
            pool = cache_k.shape[1]
            xs = _attn_decode(page_table, qbs.astype(F32), sb_bias[j],
                              cache_k[j].reshape(pool, PAGE_SIZE, D_MODEL),
                              cache_v[j].reshape(pool, PAGE_SIZE, D_MODEL))
            k_p.append(kf.reshape(n_seq, seq, N_HEADS, HEAD_DIM))
            v_p.append(vf.reshape(n_seq, seq, N_HEADS, HEAD_DIM))
            k_s.append(kfs.reshape(n_dec, 1, N_HEADS, HEAD_DIM))
            v_s.append(vfs.reshape(n_dec, 1, N_HEADS, HEAD_DIM))
            mix = (w_o[j], zero_bias, row(norm_ffn[i]))
            wr = jnp.zeros((D_MODEL, LANES), F32).at[:, :N_EXPERTS].set(w_router[j])
            moe = (wr, w_moe_gate[j], w_moe_up[j], w_moe_down[j])
            hp = _mixer_moe_routed(hp, xp, *mix, *moe, pp, *ple, final, PROMPT_ROWS)
            hs = _mixer_moe(hs, xs, *mix, *moe, ps, *ple, final, n_dec)
    return (hp.reshape(n_seq, seq, D_MODEL), hs.reshape(n_dec, 1, D_MODEL),
            jnp.stack(k_p), jnp.stack(v_p), jnp.stack(k_s), jnp.stack(v_s),
            jnp.stack(conv_p), jnp.stack(conv_s))
```

```python
import functools

import jax
import jax.numpy as jnp
from jax import lax
from jax.experimental import pallas as pl
from jax.experimental.pallas import tpu as pltpu

D_MODEL = 1024
N_HEADS = 16
HEAD_DIM = 64
SB_SCALE = HEAD_DIM ** -0.5
CONV_WIDTH = 31
D_FF = 2816
N_EXPERTS = 8
D_PLE = 256
RMS_EPS = 1e-6
LN_EPS = 1e-5
PAGE_SIZE = 128

LANES = 128
HALO_ROWS = 32
PROMPT_ROWS = 512
CONV_ROWS = 256
ATTN_ROWS = 256
FF_CHUNK = 256
MOE_CHUNKS = 2
DEC_PAGES = 8
EXPERT_ROWS = 512
CAST_ROWS = 256
VMEM_LIMIT = 56 * 1024 * 1024
LOG2E = 1.4426950408889634
MASKED = -1e30
META_E1, META_E2, META_R1, META_R2, META_W1, META_W2 = range(6)

BF16 = jnp.bfloat16
F32 = jnp.float32


def _params(n_axes):
    return pltpu.CompilerParams(dimension_semantics=("arbitrary",) * n_axes,
                                vmem_limit_bytes=VMEM_LIMIT)


def _rms(x, g):
    ms = jnp.mean(x * x, axis=-1, keepdims=True)
    return x * lax.rsqrt(ms + RMS_EPS) * g


def _sigmoid(x):
    return 1.0 / (1.0 + jnp.exp(-x))


def _const_spec(shape):
    zeros = (0,) * len(shape)
    return pl.BlockSpec(shape, lambda *_: zeros)


def _cast_kernel(x_ref, o_ref):
    o_ref[...] = x_ref[...].astype(BF16)


def _to_bf16(w):
    rows, cols = w.shape[-2:]
    w3 = w.reshape(-1, rows, cols)
    spec = pl.BlockSpec((None, CAST_ROWS, cols), lambda a, r: (a, r, 0))
    out = pl.pallas_call(
        _cast_kernel,
        grid=(w3.shape[0], rows // CAST_ROWS),
        in_specs=[spec],
        out_specs=spec,
        out_shape=jax.ShapeDtypeStruct(w3.shape, BF16),
        compiler_params=_params(2),
        name="cast_bf16",
    )(w3)
    return out.reshape(w.shape)


def _glu_kernel(h_ref, g_ref, w_ref, b_ref, o_ref):
    u = _rms(h_ref[...], g_ref[...]).astype(BF16)
    a = jnp.dot(u, w_ref[...], preferred_element_type=F32) + b_ref[...]
    o_ref[...] = a[:, :D_MODEL] * _sigmoid(a[:, D_MODEL:])


def _glu(h, g, w, b, tm):
    m = h.shape[0]
    return pl.pallas_call(
        _glu_kernel,
        grid=(m // tm,),
        in_specs=[pl.BlockSpec((tm, D_MODEL), lambda i: (i, 0)),
                  _const_spec((1, D_MODEL)),
                  _const_spec((D_MODEL, 2 * D_MODEL)),
                  _const_spec((1, 2 * D_MODEL))],
        out_specs=pl.BlockSpec((tm, D_MODEL), lambda i: (i, 0)),
        out_shape=jax.ShapeDtypeStruct((m, D_MODEL), F32),
        compiler_params=_params(1),
        name="glu",
    )(h, g, w, b)


def _ln_swish(y, lg, lb):
    mu = jnp.mean(y, axis=-1, keepdims=True)
    yc = y - mu
    var = jnp.mean(yc * yc, axis=-1, keepdims=True)
    yn = yc * lax.rsqrt(var + LN_EPS) * lg + lb
    return yn * _sigmoid(yn)


def _dwconv_kernel(halo_ref, x_ref, wd_ref, bd_ref, lg_ref, lb_ref, o_ref, buf, sh, ybuf):
    tq = CONV_ROWS
    i = pl.program_id(1)
    buf[0:HALO_ROWS, :] = jnp.where(i > 0, halo_ref[...], 0.0)
    buf[HALO_ROWS:HALO_ROWS + tq, :] = x_ref[...]
    base = HALO_ROWS - (CONV_WIDTH - 1)
    n_taps = [len(range(r, CONV_WIDTH, 8)) for r in range(8)]
    for r in range(8):
        n = tq + 8 * (n_taps[r] - 1)
        sh[r, 0:n, :] = buf[base + r:base + r + n, :]

    rb = 64

    def col_body(c, carry):
        cs = pl.ds(pl.multiple_of(c * LANES, LANES), LANES)
        for blk in range(tq // rb):
            acc = jnp.zeros((rb, LANES), F32) + bd_ref[:, cs]
            for r in range(8):
                for a in range(n_taps[r]):
                    w = 8 * a + r
                    lo = 8 * a + blk * rb
                    acc = acc + sh[r, lo:lo + rb, cs] * wd_ref[w:w + 1, cs]
            ybuf[blk * rb:(blk + 1) * rb, cs] = acc
        return carry

    lax.fori_loop(0, D_MODEL // LANES, col_body, 0)
    o_ref[...] = _ln_swish(ybuf[...], lg_ref[...], lb_ref[...]).astype(BF16)


def _dwconv_prompt(glu, n_seq, seq, wd, bd, lg, lb):
    tq = CONV_ROWS
    nq = seq // tq
    halo_per_blk = tq // HALO_ROWS

    def halo_map(b, i):
        return (jnp.maximum(b * (seq // HALO_ROWS) + i * halo_per_blk - 1, 0), 0)

    return pl.pallas_call(
        _dwconv_kernel,
        grid=(n_seq, nq),
        in_specs=[pl.BlockSpec((HALO_ROWS, D_MODEL), halo_map),
                  pl.BlockSpec((tq, D_MODEL), lambda b, i: (b * nq + i, 0)),
                  _const_spec((CONV_WIDTH, D_MODEL)),
                  _const_spec((1, D_MODEL)),
                  _const_spec((1, D_MODEL)),
                  _const_spec((1, D_MODEL))],
        out_specs=pl.BlockSpec((tq, D_MODEL), lambda b, i: (b * nq + i, 0)),
        out_shape=jax.ShapeDtypeStruct((n_seq * seq, D_MODEL), BF16),
        scratch_shapes=[pltpu.VMEM((HALO_ROWS + tq, D_MODEL), F32),
                        pltpu.VMEM((8, tq + 24, D_MODEL), F32),
                        pltpu.VMEM((tq, D_MODEL), F32)],
        compiler_params=_params(2),
        name="dwconv_prompt",
    )(glu, glu, wd, bd, lg, lb)


def _dwconv_sample_kernel(hist_ref, x_ref, wd_ref, bd_ref, lg_ref, lb_ref, o_ref):
    acc = x_ref[...] * wd_ref[CONV_WIDTH - 1:CONV_WIDTH, :] + bd_ref[...]
    for w in range(CONV_WIDTH - 1):
        acc = acc + hist_ref[w] * wd_ref[w:w + 1, :]
    o_ref[...] = _ln_swish(acc, lg_ref[...], lb_ref[...]).astype(BF16)


def _dwconv_sample(hist_t, glu, wd, bd, lg, lb):
    n = glu.shape[0]
    return pl.pallas_call(
        _dwconv_sample_kernel,
        grid=(1,),
        in_specs=[_const_spec((CONV_WIDTH - 1, n, D_MODEL)),
                  _const_spec((n, D_MODEL)),
                  _const_spec((CONV_WIDTH, D_MODEL)),
                  _const_spec((1, D_MODEL)),
                  _const_spec((1, D_MODEL)),
                  _const_spec((1, D_MODEL))],
        out_specs=_const_spec((n, D_MODEL)),
        out_shape=jax.ShapeDtypeStruct((n, D_MODEL), BF16),
        compiler_params=_params(1),
        name="dwconv_sample",
    )(hist_t, glu, wd, bd, lg, lb)


def _qkv_kernel(h_ref, g_ref, w_ref, kf_ref, vf_ref, qb_ref, kb_ref, vb_ref):
    u = _rms(h_ref[...], g_ref[...]).astype(BF16)
    qkv = jnp.dot(u, w_ref[...], preferred_element_type=F32)
    k = qkv[:, D_MODEL:2 * D_MODEL]
    v = qkv[:, 2 * D_MODEL:]
    kf_ref[...] = k
    vf_ref[...] = v
    qb_ref[...] = (qkv[:, :D_MODEL] * (SB_SCALE * LOG2E)).astype(BF16)
    kb_ref[...] = k.astype(BF16)
    vb_ref[...] = v.astype(BF16)


def _qkv(h, g, w, tm):
    m = h.shape[0]
    row = pl.BlockSpec((tm, D_MODEL), lambda i: (i, 0))
    return pl.pallas_call(
        _qkv_kernel,
        grid=(m // tm,),
        in_specs=[row, _const_spec((1, D_MODEL)), _const_spec((D_MODEL, 3 * D_MODEL))],
        out_specs=[row] * 5,
        out_shape=[jax.ShapeDtypeStruct((m, D_MODEL), F32)] * 2
        + [jax.ShapeDtypeStruct((m, D_MODEL), BF16)] * 3,
        compiler_params=_params(1),
        name="qkv",
    )(h, g, w)


def _qkv_seq_kernel(h_ref, g_ref, w_ref, wt_ref, kt_ref, vt_ref, qb_ref, kb_ref, vb_ref):
    u = _rms(h_ref[...], g_ref[...]).astype(BF16)
    qkv = jnp.dot(u, w_ref[...], preferred_element_type=F32)
    qb_ref[...] = (qkv[:, :D_MODEL] * (SB_SCALE * LOG2E)).astype(BF16)
    kb_ref[...] = qkv[:, D_MODEL:2 * D_MODEL].astype(BF16)
    vb_ref[...] = qkv[:, 2 * D_MODEL:].astype(BF16)
    kvt = lax.dot_general(wt_ref[...], u, (((1,), (1,)), ((), ())), preferred_element_type=F32)
    kt_ref[...] = kvt[:D_MODEL]
    vt_ref[...] = kvt[D_MODEL:]


def _qkv_seq(h, g, w, n_seq, tm):
    m = h.shape[0]
    seq = m // n_seq
    blocks = seq // tm
    wt = jnp.swapaxes(w[:, D_MODEL:], 0, 1)
    row = pl.BlockSpec((tm, D_MODEL), lambda i: (i, 0))
    col = pl.BlockSpec((None, D_MODEL, tm), lambda i: (i // blocks, 0, i % blocks))
    return pl.pallas_call(
        _qkv_seq_kernel,
        grid=(m // tm,),
        in_specs=[row, _const_spec((1, D_MODEL)), _const_spec((D_MODEL, 3 * D_MODEL)),
                  _const_spec((2 * D_MODEL, D_MODEL))],
        out_specs=[col, col, row, row, row],
        out_shape=[jax.ShapeDtypeStruct((n_seq, D_MODEL, seq), F32)] * 2
        + [jax.ShapeDtypeStruct((m, D_MODEL), BF16)] * 3,
        compiler_params=_params(1),
        name="qkv_seq",
    )(h, g, w, wt)


def _softplus2(y):
    neg_abs = lax.bitcast_convert_type(
        lax.bitcast_convert_type(y, jnp.uint32) | jnp.uint32(0x80000000), F32)
    return jnp.maximum(y, 0.0) + jnp.log(1.0 + jnp.exp2(neg_abs)) * LOG2E


def _attn_kernel(bias_ref, q_ref, k_ref, v_ref, t_ref, o_ref, *scratch):
    t = ATTN_ROWS
    nq = q_ref.shape[0] // t
    n_tiles = nq * (nq + 1) // 2
    hp = pl.program_id(1)
    heads = (scratch[:7], scratch[7:])
    lane = lax.broadcasted_iota(jnp.int32, (1, LANES), 1)
    first = lane < HEAD_DIM
    q = q_ref[...]
    row = lax.broadcasted_iota(jnp.int32, (t, t), 0)
    col = lax.broadcasted_iota(jnp.int32, (t, t), 1)
    for hh, (qm, bm, o_acc, c_acc, y_buf, sp_buf, a_buf) in enumerate(heads):
        keep = first if hh == 0 else jnp.logical_not(first)
        qm[...] = jnp.where(keep, q, jnp.zeros_like(q))
        bias = jnp.full((t, t), bias_ref[2 * hp + hh], F32)
        bm[0] = bias
        bm[1] = jnp.where(col < row, bias, MASKED)
        bm[2] = jnp.full((t, t), MASKED, F32)
        o_acc[...] = jnp.zeros_like(o_acc)
        c_acc[...] = jnp.zeros_like(c_acc)
        y_buf[...] = jnp.full(y_buf.shape, MASKED, F32)
        sp_buf[...] = jnp.zeros_like(sp_buf)
        a_buf[...] = jnp.zeros_like(a_buf)

    def rows(blk):
        return pl.ds(pl.multiple_of(blk * t, t), t)

    def body(f, tiles):
        (i0, j0), _, (i2, _), (i3, j3) = tiles
        kind = jnp.where(f >= n_tiles, 2, jnp.where(j0 == i0, 1, 0))
        y0, y1, y2 = lax.rem(f, 3), lax.rem(f + 2, 3), lax.rem(f + 1, 3)
        even, odd = lax.rem(f, 2), lax.rem(f + 1, 2)
        kblk = k_ref[rows(j0), :]
        vblk = v_ref[rows(j3), :]
        for qm, bm, o_acc, c_acc, y_buf, sp_buf, a_buf in heads:
            o_acc[rows(i3), :] += jnp.dot(a_buf[odd], vblk, preferred_element_type=F32)
            cum = jnp.dot(sp_buf[even], t_ref[...], preferred_element_type=F32)
            c = c_acc[i2]
            a_buf[even] = jnp.exp2(y_buf[y2] - cum - c).astype(BF16)
            c_acc[i2] = c + cum[:, 0:1]
            sp_buf[odd] = _softplus2(y_buf[y1]).astype(BF16)
            y_buf[y0] = lax.dot_general(
                qm[rows(i0), :], kblk, (((1,), (1,)), ((), ())),
                preferred_element_type=F32) + bm[kind]
        wrap = j0 == 0
        i_next = jnp.minimum(jnp.where(wrap, i0 + 1, i0), nq - 1)
        j_next = jnp.minimum(jnp.where(wrap, i0 + 1, j0 - 1), nq - 1)
        return ((i_next, j_next),) + tiles[:3]

    zero = jnp.int32(0)
    lax.fori_loop(0, n_tiles + 3, body, ((zero, zero),) * 4)
    o_ref[...] = jnp.where(first, heads[0][2][...], heads[1][2][...]).astype(BF16)


def _attn_prompt(sb_bias, qb, kb, vb, n_seq, seq):
    t = ATTN_ROWS
    nq = seq // t
    tmat = (jnp.arange(t)[:, None] >= jnp.arange(t)[None, :]).astype(BF16)
    seq_spec = pl.BlockSpec((None, seq, LANES), lambda b, hp: (b, 0, hp))
    per_head = [pltpu.VMEM((seq, LANES), BF16),
                pltpu.VMEM((3, t, t), F32),
                pltpu.VMEM((seq, LANES), F32),
                pltpu.VMEM((nq, t, 1), F32),
                pltpu.VMEM((3, t, t), F32),
                pltpu.VMEM((2, t, t), BF16),
                pltpu.VMEM((2, t, t), BF16)]
    out = pl.pallas_call(
        _attn_kernel,
        grid=(n_seq, N_HEADS // 2),
        in_specs=[pl.BlockSpec(memory_space=pltpu.SMEM),
                  seq_spec, seq_spec, seq_spec,
                  _const_spec((t, t))],
        out_specs=seq_spec,
        out_shape=jax.ShapeDtypeStruct((n_seq, seq, D_MODEL), BF16),
        scratch_shapes=per_head * 2,
        compiler_params=_params(2),
        name="attn_prompt",
    )(sb_bias * LOG2E, qb.reshape(n_seq, seq, D_MODEL), kb.reshape(n_seq, seq, D_MODEL),
      vb.reshape(n_seq, seq, D_MODEL), tmat)
    return out.reshape(n_seq * seq, D_MODEL)


def _decode_kernel(pt_ref, qbd_ref, bias_ref, t_ref, *refs):
    k_refs = refs[:DEC_PAGES]
    v_refs = refs[DEC_PAGES:2 * DEC_PAGES]
    o_ref, acc, carry = refs[2 * DEC_PAGES:]
    g = pl.program_id(1)

    @pl.when(g == 0)
    def _():
        acc[...] = jnp.zeros_like(acc)
        carry[...] = jnp.zeros_like(carry)

    qbd = qbd_ref[...]
    for s in range(DEC_PAGES):
        y = jnp.dot(qbd, k_refs[s][...].astype(BF16), preferred_element_type=F32) + bias_ref[...]
        sp = _softplus2(y)
        cum = jnp.dot(sp.astype(BF16), t_ref[...], preferred_element_type=F32)
        a = jnp.exp2(y - cum - carry[...])
        carry[...] += cum[:, 0:1]
        a_rows = jnp.broadcast_to(a[:, None, :], (N_HEADS, HEAD_DIM, PAGE_SIZE))
        acc[...] += v_refs[s][...] * a_rows.reshape(D_MODEL, PAGE_SIZE)

    @pl.when(g == pl.num_programs(1) - 1)
    def _():
        ones = jnp.ones((8, PAGE_SIZE), F32)
        sums = lax.dot_general(ones, acc[...], (((1,), (1,)), ((), ())),
                               preferred_element_type=F32, precision=lax.Precision.HIGHEST)
        o_ref[...] = sums[0:1, :]


def _attn_decode(page_table, q_scaled, sb_bias, cache_k, cache_v, layer):
    n, n_pages = page_table.shape
    n_layers, pool = cache_k.shape[:2]
    by_position = lambda c: jnp.transpose(c, (0, 1, 3, 4, 2)).reshape(
        n_layers, pool, D_MODEL, PAGE_SIZE)
    cache_k, cache_v = by_position(cache_k), by_position(cache_v)
    head_of_col = jnp.arange(D_MODEL) // HEAD_DIM
    own = jnp.arange(N_HEADS)[:, None] == head_of_col[None, :]
    qbd = jnp.where(own[None], q_scaled[:, None, :], 0.0).astype(BF16)
    bias = jnp.broadcast_to((sb_bias * LOG2E)[:, None], (N_HEADS, PAGE_SIZE))
    tmat = (jnp.arange(PAGE_SIZE)[:, None] >= jnp.arange(PAGE_SIZE)[None, :]).astype(BF16)
    steps = n_pages // DEC_PAGES

    def page_spec(s):
        def index_map(b, g, pt):
            return (layer, pt[b, n_pages - 1 - g * DEC_PAGES - s], 0, 0)
        return pl.BlockSpec((None, None, D_MODEL, PAGE_SIZE), index_map)

    grid_spec = pltpu.PrefetchScalarGridSpec(
        num_scalar_prefetch=1,
        grid=(n, steps),
        in_specs=[pl.BlockSpec((None, N_HEADS, D_MODEL), lambda b, g, pt: (b, 0, 0)),
                  pl.BlockSpec((N_HEADS, PAGE_SIZE), lambda b, g, pt: (0, 0)),
                  pl.BlockSpec((PAGE_SIZE, PAGE_SIZE), lambda b, g, pt: (0, 0))]
        + [page_spec(s) for s in range(DEC_PAGES)] * 2,
        out_specs=pl.BlockSpec((None, 1, D_MODEL), lambda b, g, pt: (b, 0, 0)),
        scratch_shapes=[pltpu.VMEM((D_MODEL, PAGE_SIZE), F32), pltpu.VMEM((N_HEADS, 1), F32)],
    )
    out = pl.pallas_call(
        _decode_kernel,
        grid_spec=grid_spec,
        out_shape=jax.ShapeDtypeStruct((n, 1, D_MODEL), F32),
        compiler_params=_params(2),
        name="attn_decode",
    )(page_table, qbd, bias, tmat, *([cache_k] * DEC_PAGES), *([cache_v] * DEC_PAGES))
    return out.reshape(n, D_MODEL)


def _ple_tail(hres, p_ref, gple_ref, wgate_ref, wproj_ref, gfin_ref, final):
    r = _rms(hres, gple_ref[...]).astype(BF16)
    gate = _sigmoid(jnp.dot(r, wgate_ref[...], preferred_element_type=F32))
    proj = jnp.dot(p_ref[...].astype(BF16), wproj_ref[...], preferred_element_type=F32)
    out = hres + gate * proj
    if final:
        out = _rms(out, gfin_ref[...])
    return out


def _mix_in(h_ref, x_ref, wmix_ref, bmix_ref):
    return h_ref[...] + jnp.dot(x_ref[...].astype(BF16), wmix_ref[...],
                                preferred_element_type=F32) + bmix_ref[...]


def _silu_mul(g, u):
    return (g * _sigmoid(g) * u).astype(BF16)


def _mixer_dense_kernel(h_ref, x_ref, wmix_ref, bmix_ref, gffn_ref, wg_ref, wu_ref, wd_ref,
                        p_ref, gple_ref, wgate_ref, wproj_ref, gfin_ref, o_ref, *, final):
    h1 = _mix_in(h_ref, x_ref, wmix_ref, bmix_ref)
    f = _rms(h1, gffn_ref[...]).astype(BF16)
    acc = h1
    for c in range(D_FF // FF_CHUNK):
        cs = slice(c * FF_CHUNK, (c + 1) * FF_CHUNK)
        g = jnp.dot(f, wg_ref[:, cs], preferred_element_type=F32)
        u = jnp.dot(f, wu_ref[:, cs], preferred_element_type=F32)
        acc = acc + jnp.dot(_silu_mul(g, u), wd_ref[cs, :], preferred_element_type=F32)
    o_ref[...] = _ple_tail(acc, p_ref, gple_ref, wgate_ref, wproj_ref, gfin_ref, final)


def _single(shape):
    zeros = (0,) * len(shape)
    return pl.BlockSpec(shape, lambda *_: zeros, pipeline_mode=pl.Buffered(1))


def _mixer_dense(h, x, wmix, bmix, gffn, wg, wu, wd, p, gple, wgate, wproj, gfin, final, tm):
    m = h.shape[0]
    row = lambda width: pl.BlockSpec((tm, width), lambda i: (i, 0))
    return pl.pallas_call(
        functools.partial(_mixer_dense_kernel, final=final),
        grid=(m // tm,),
        in_specs=[row(D_MODEL), row(D_MODEL),
                  _single((D_MODEL, D_MODEL)), _single((1, D_MODEL)), _single((1, D_MODEL)),
                  _single((D_MODEL, D_FF)), _single((D_MODEL, D_FF)), _single((D_FF, D_MODEL)),
                  row(D_PLE), _single((1, D_MODEL)),
                  _single((D_MODEL, D_MODEL)), _single((D_PLE, D_MODEL)), _single((1, D_MODEL))],
        out_specs=row(D_MODEL),
        out_shape=jax.ShapeDtypeStruct((m, D_MODEL), F32),
        compiler_params=_params(1),
        name="mixer_dense",
    )(h, x, wmix, bmix, gffn, wg, wu, wd, p, gple, wgate, wproj, gfin)


def _top2(logits):
    lane = lax.broadcasted_iota(jnp.int32, logits.shape, 1).astype(F32)
    neg = -jnp.inf
    lg = jnp.where(lane < N_EXPERTS, logits, neg)
    m1 = jnp.max(lg, axis=1, keepdims=True)
    i1 = jnp.min(jnp.where(lg == m1, lane, float(LANES)), axis=1, keepdims=True)
    lg2 = jnp.where(lane == i1, neg, lg)
    m2 = jnp.max(lg2, axis=1, keepdims=True)
    i2 = jnp.min(jnp.where(lg2 == m2, lane, float(LANES)), axis=1, keepdims=True)
    t = jnp.exp(m2 - m1)
    return lane, i1, i2, 1.0 / (1.0 + t), t / (1.0 + t)


def _router_logits(f, wr_ref):
    return jnp.dot(f, wr_ref[...], preferred_element_type=F32, precision=lax.Precision.HIGHEST)


def _mixer_moe_kernel(h_ref, x_ref, wmix_ref, bmix_ref, gffn_ref, wr_ref, wg_ref, wu_ref, wd_ref,
                      p_ref, gple_ref, wgate_ref, wproj_ref, gfin_ref, o_ref,
                      f_scr, gate_scr, acc_scr, *, final):
    e = pl.program_id(1)
    c = pl.program_id(2)

    @pl.when((e == 0) & (c == 0))
    def _():
        h1 = _mix_in(h_ref, x_ref, wmix_ref, bmix_ref)
        f = _rms(h1, gffn_ref[...])
        lane, i1, i2, w1, w2 = _top2(_router_logits(f, wr_ref))
        gate_scr[...] = jnp.where(lane == i1, w1, 0.0) + jnp.where(lane == i2, w2, 0.0)
        f_scr[...] = f.astype(BF16)
        acc_scr[...] = h1

    f = f_scr[...]
    g = jnp.dot(f, wg_ref[...], preferred_element_type=F32)
    u = jnp.dot(f, wu_ref[...], preferred_element_type=F32)
    y = jnp.dot(_silu_mul(g, u), wd_ref[...], preferred_element_type=F32)
    gate = gate_scr[...]
    lane = lax.broadcasted_iota(jnp.int32, gate.shape, 1)
    ge = jnp.sum(jnp.where(lane == e, gate, 0.0), axis=1, keepdims=True)
    acc_scr[...] += ge * y

    @pl.when((e == pl.num_programs(1) - 1) & (c == pl.num_programs(2) - 1))
    def _():
        o_ref[...] = _ple_tail(acc_scr[...], p_ref, gple_ref, wgate_ref, wproj_ref, gfin_ref, final)


def _mixer_moe(h, x, wmix, bmix, gffn, wr, wg, wu, wd, p, gple, wgate, wproj, gfin, final, tm):
    m = h.shape[0]
    fc = D_FF // MOE_CHUNKS
    row = lambda width: pl.BlockSpec((tm, width), lambda i, e, c: (i, 0))
    return pl.pallas_call(
        functools.partial(_mixer_moe_kernel, final=final),
        grid=(m // tm, N_EXPERTS, MOE_CHUNKS),
        in_specs=[row(D_MODEL), row(D_MODEL),
                  _single((D_MODEL, D_MODEL)), _single((1, D_MODEL)), _single((1, D_MODEL)),
                  _single((D_MODEL, LANES)),
                  pl.BlockSpec((None, D_MODEL, fc), lambda i, e, c: (e, 0, c)),
                  pl.BlockSpec((None, D_MODEL, fc), lambda i, e, c: (e, 0, c)),
                  pl.BlockSpec((None, fc, D_MODEL), lambda i, e, c: (e, c, 0)),
                  row(D_PLE), _single((1, D_MODEL)),
                  _single((D_MODEL, D_MODEL)), _single((D_PLE, D_MODEL)), _single((1, D_MODEL))],
        out_specs=row(D_MODEL),
        out_shape=jax.ShapeDtypeStruct((m, D_MODEL), F32),
        scratch_shapes=[pltpu.VMEM((tm, D_MODEL), BF16),
                        pltpu.VMEM((tm, LANES), F32),
                        pltpu.VMEM((tm, D_MODEL), F32)],
        compiler_params=_params(3),
        name="mixer_moe",
    )(h, x, wmix, bmix, gffn, wr, wg, wu, wd, p, gple, wgate, wproj, gfin)


def _moe_route_kernel(h_ref, x_ref, wmix_ref, bmix_ref, gffn_ref, wr_ref, ltri_ref,
                      h1_ref, f_ref, meta_ref, cnt_ref):
    h1 = _mix_in(h_ref, x_ref, wmix_ref, bmix_ref)
    f = _rms(h1, gffn_ref[...])
    lane, i1, i2, w1, w2 = _top2(_router_logits(f, wr_ref))
    sel = jnp.where(lane == i1, 1.0, 0.0) + jnp.where(lane == i2, 1.0, 0.0)
    rank = jnp.dot(ltri_ref[...], sel.astype(BF16), preferred_element_type=F32)
    r1 = jnp.sum(jnp.where(lane == i1, rank, 0.0), axis=1, keepdims=True)
    r2 = jnp.sum(jnp.where(lane == i2, rank, 0.0), axis=1, keepdims=True)
    meta = jnp.zeros_like(rank)
    for pos, val in ((META_E1, i1), (META_E2, i2), (META_R1, r1), (META_R2, r2),
                     (META_W1, w1), (META_W2, w2)):
        meta = jnp.where(lane == float(pos), val, meta)
    h1_ref[...] = h1
    f_ref[...] = f
    meta_ref[...] = meta
    cnt_ref[...] = jnp.sum(sel, axis=0, keepdims=True)


def _moe_route(h, x, wmix, bmix, gffn, wr, tm):
    m = h.shape[0]
    ltri = (jnp.arange(tm)[:, None] > jnp.arange(tm)[None, :]).astype(BF16)
    row = lambda width: pl.BlockSpec((tm, width), lambda i: (i, 0))
    return pl.pallas_call(
        _moe_route_kernel,
        grid=(m // tm,),
        in_specs=[row(D_MODEL), row(D_MODEL),
                  _single((D_MODEL, D_MODEL)), _single((1, D_MODEL)), _single((1, D_MODEL)),
                  _single((D_MODEL, LANES)), _single((tm, tm))],
        out_specs=[row(D_MODEL), row(D_MODEL), row(LANES),
                   pl.BlockSpec((None, 1, LANES), lambda i: (i, 0, 0))],
        out_shape=[jax.ShapeDtypeStruct((m, D_MODEL), F32), jax.ShapeDtypeStruct((m, D_MODEL), F32),
                   jax.ShapeDtypeStruct((m, LANES), F32),
                   jax.ShapeDtypeStruct((m // tm, 1, LANES), F32)],
        compiler_params=_params(1),
        name="moe_route",
    )(h, x, wmix, bmix, gffn, wr, ltri)


def _route_plan(meta, cnt, tm):
    m = meta.shape[0]
    n_rows = 2 * m + N_EXPERTS * EXPERT_ROWS
    expert = meta[:, META_E1:META_E2 + 1].astype(jnp.int32)
    rank = meta[:, META_R1:META_R2 + 1].astype(jnp.int32)
    cnt = cnt[:, 0, :N_EXPERTS].astype(jnp.int32)
    total = jnp.sum(cnt, axis=0)
    region = (total + EXPERT_ROWS - 1) // EXPERT_ROWS * EXPERT_ROWS
    region_end = jnp.cumsum(region)
    base = (region_end - region)[None, :] + jnp.cumsum(cnt, axis=0) - cnt
    base_rows = jnp.repeat(base, tm, axis=0)[:, None, :]
    hit = expert[:, :, None] == jnp.arange(N_EXPERTS)[None, None, :]
    dest = (jnp.sum(jnp.where(hit, base_rows, 0), axis=2) + rank).reshape(-1)
    starts = jnp.arange(n_rows // EXPERT_ROWS) * EXPERT_ROWS
    tile_expert = jnp.minimum(jnp.sum(starts[:, None] >= region_end[None, :], axis=1),
                              N_EXPERTS - 1).astype(jnp.int32)
    active = (starts < region_end[-1]).astype(jnp.int32)
    return dest.astype(jnp.int32), tile_expert, active, n_rows


def _row_copies(idx_ref, n, make):
    def start(m, carry):
        for k in range(2):
            make(m, k, idx_ref[2 * m + k]).start()
        return carry

    def wait(m, carry):
        for k in range(2):
            make(m, k, idx_ref[2 * m + k]).wait()
        return carry

    lax.fori_loop(0, n, start, 0)
    lax.fori_loop(0, n, wait, 0)


def _moe_scatter_kernel(dest_ref, f_ref, zeros_ref, xs_ref, sem):
    del zeros_ref

    def make(m, k, d):
        return pltpu.make_async_copy(f_ref.at[pl.ds(m, 1)], xs_ref.at[pl.ds(d, 1)], sem)

    _row_copies(dest_ref, f_ref.shape[0], make)


def _moe_scatter(dest, f, n_rows, tm):
    m = f.shape[0]
    return pl.pallas_call(
        _moe_scatter_kernel,
        grid=(m // tm,),
        in_specs=[pl.BlockSpec((2 * tm,), lambda i: (i,), memory_space=pltpu.SMEM),
                  pl.BlockSpec((tm, D_MODEL), lambda i: (i, 0)),
                  pl.BlockSpec(memory_space=pl.ANY)],
        out_specs=pl.BlockSpec(memory_space=pl.ANY),
        out_shape=jax.ShapeDtypeStruct((n_rows, D_MODEL), F32),
        scratch_shapes=[pltpu.SemaphoreType.DMA(())],
        input_output_aliases={2: 0},
        compiler_params=_params(1),
        name="moe_scatter",
    )(dest, f, jnp.zeros((n_rows, D_MODEL), F32))


def _moe_expert_kernel(expert_ref, active_ref, x_ref, wg_ref, wu_ref, wd_ref, o_ref):
    del expert_ref
    k = pl.program_id(0)

    @pl.when(active_ref[k] != 0)
    def _():
        x = x_ref[...].astype(BF16)
        acc = None
        for c in range(D_FF // FF_CHUNK):
            cs = slice(c * FF_CHUNK, (c + 1) * FF_CHUNK)
            g = jnp.dot(x, wg_ref[:, cs], preferred_element_type=F32)
            u = jnp.dot(x, wu_ref[:, cs], preferred_element_type=F32)
            y = jnp.dot(_silu_mul(g, u), wd_ref[cs, :], preferred_element_type=F32)
            acc = y if acc is None else acc + y
        o_ref[...] = acc

    @pl.when(active_ref[k] == 0)
    def _():
        o_ref[...] = jnp.zeros_like(o_ref)


def _moe_experts(tile_expert, active, xs, wg, wu, wd):
    n_rows = xs.shape[0]
    rows = pl.BlockSpec((EXPERT_ROWS, D_MODEL), lambda k, te, ac: (k, 0))
    grid_spec = pltpu.PrefetchScalarGridSpec(
        num_scalar_prefetch=2,
        grid=(n_rows // EXPERT_ROWS,),
        in_specs=[rows,
                  pl.BlockSpec((None, D_MODEL, D_FF), lambda k, te, ac: (te[k], 0, 0)),
                  pl.BlockSpec((None, D_MODEL, D_FF), lambda k, te, ac: (te[k], 0, 0)),
                  pl.BlockSpec((None, D_FF, D_MODEL), lambda k, te, ac: (te[k], 0, 0))],
        out_specs=rows,
    )
    return pl.pallas_call(
        _moe_expert_kernel,
        grid_spec=grid_spec,
        out_shape=jax.ShapeDtypeStruct((n_rows, D_MODEL), F32),
        compiler_params=_params(1),
        name="moe_experts",
    )(tile_expert, active, xs, wg, wu, wd)


def _moe_combine_kernel(dest_ref, ys_ref, h1_ref, meta_ref, p_ref, gple_ref, wgate_ref, wproj_ref,
                        gfin_ref, o_ref, ybuf, sem, *, final):
    def make(m, k, d):
        return pltpu.make_async_copy(ys_ref.at[pl.ds(d, 1)], ybuf.at[k, pl.ds(m, 1)], sem)

    _row_copies(dest_ref, h1_ref.shape[0], make)
    meta = meta_ref[...]
    lane = lax.broadcasted_iota(jnp.int32, meta.shape, 1)
    w1 = jnp.sum(jnp.where(lane == META_W1, meta, 0.0), axis=1, keepdims=True)
    w2 = jnp.sum(jnp.where(lane == META_W2, meta, 0.0), axis=1, keepdims=True)
    hres = h1_ref[...] + (w1 * ybuf[0] + w2 * ybuf[1])
    o_ref[...] = _ple_tail(hres, p_ref, gple_ref, wgate_ref, wproj_ref, gfin_ref, final)


def _moe_combine(dest, ys, h1, meta, p, gple, wgate, wproj, gfin, final, tm):
    m = h1.shape[0]
    row = lambda width: pl.BlockSpec((tm, width), lambda i: (i, 0))
    return pl.pallas_call(
        functools.partial(_moe_combine_kernel, final=final),
        grid=(m // tm,),
        in_specs=[pl.BlockSpec((2 * tm,), lambda i: (i,), memory_space=pltpu.SMEM),
                  pl.BlockSpec(memory_space=pl.ANY),
                  row(D_MODEL), row(LANES), row(D_PLE), _single((1, D_MODEL)),
                  _single((D_MODEL, D_MODEL)), _single((D_PLE, D_MODEL)), _single((1, D_MODEL))],
        out_specs=row(D_MODEL),
        out_shape=jax.ShapeDtypeStruct((m, D_MODEL), F32),
        scratch_shapes=[pltpu.VMEM((2, tm, D_MODEL), F32), pltpu.SemaphoreType.DMA(())],
        compiler_params=_params(1),
        name="moe_combine",
    )(dest, ys, h1, meta, p, gple, wgate, wproj, gfin)


def _mixer_moe_routed(h, x, wmix, bmix, gffn, wr, wg, wu, wd, p, gple, wgate, wproj, gfin, final, tm):
    h1, f, meta, cnt = _moe_route(h, x, wmix, bmix, gffn, wr, tm)
    dest, tile_expert, active, n_rows = _route_plan(meta, cnt, tm)
    xs = _moe_scatter(dest, f, n_rows, tm)
    ys = _moe_experts(tile_expert, active, xs, wg, wu, wd)
    return _moe_combine(dest, ys, h1, meta, p, gple, wgate, wproj, gfin, final, tm)


def kernel(x_prompt, x_sample, cache_k, cache_v, state_conv, page_table, p_prompt, p_sample, norm_mix, w_pw1, b_pw1, w_dw, b_dw, ln_conv_g, ln_conv_b, w_pw2, b_pw2, w_qkv, w_o, sb_bias, norm_ffn, w_ffn_gate, w_ffn_up, w_ffn_down, w_router, w_moe_gate, w_moe_up, w_moe_down, norm_ple, w_ple_gate, w_ple_proj, norm_final):
    n_seq, seq, _ = x_prompt.shape
    n_dec = x_sample.shape[0]
    depth = norm_mix.shape[0]
    mp = n_seq * seq
    hp = x_prompt.reshape(mp, D_MODEL)
    hs = x_sample.reshape(n_dec, D_MODEL)
    row = lambda a: a.reshape(1, -1)
    zero_bias = jnp.zeros((1, D_MODEL), F32)
    gfin = row(norm_final)
    w_pw1, w_pw2, w_qkv, w_o = map(_to_bf16, (w_pw1, w_pw2, w_qkv, w_o))
    w_ffn_gate, w_ffn_up, w_ffn_down = map(_to_bf16, (w_ffn_gate, w_ffn_up, w_ffn_down))
    w_moe_gate, w_moe_up, w_moe_down = map(_to_bf16, (w_moe_gate, w_moe_up, w_moe_down))
    w_ple_gate, w_ple_proj = map(_to_bf16, (w_ple_gate, w_ple_proj))
    last_rows = CONV_WIDTH - 1

    k_p, v_p, k_s, v_s, conv_p, conv_s = [], [], [], [], [], []
    for i in range(depth):
        j = i // 2
        final = i == depth - 1
        gmix = row(norm_mix[i])
        ple = (row(norm_ple[i]), w_ple_gate[i], w_ple_proj[i], gfin)
        pp = p_prompt[i].reshape(mp, D_PLE)
        ps = p_sample[i].reshape(n_dec, D_PLE)
        if i % 2 == 0:
            w1 = w_pw1[j]
            b1 = row(b_pw1[j])
            conv_w = (w_dw[j], row(b_dw[j]), row(ln_conv_g[j]), row(ln_conv_b[j]))
            glu_p = _glu(hp, gmix, w1, b1, PROMPT_ROWS)
            glu_s = _glu(hs, gmix, w1, b1, n_dec)
            xp = _dwconv_prompt(glu_p, n_seq, seq, *conv_w)
            xs = _dwconv_sample(jnp.swapaxes(state_conv[j], 0, 1), glu_s, *conv_w)
            conv_p.append(glu_p.reshape(n_seq, seq, D_MODEL)[:, seq - last_rows:])
            conv_s.append(jnp.concatenate([state_conv[j][:, 1:], glu_s[:, None, :]], axis=1))
            mix = (w_pw2[j], row(b_pw2[j]), row(norm_ffn[i]))
            ffn = (w_ffn_gate[j], w_ffn_up[j], w_ffn_down[j])
            hp = _mixer_dense(hp, xp, *mix, *ffn, pp, *ple, final, PROMPT_ROWS)
            hs = _mixer_dense(hs, xs, *mix, *ffn, ps, *ple, final, n_dec)
        else:
            wq = w_qkv[j]
            kt, vt, qb, kb, vb = _qkv_seq(hp, gmix, wq, n_seq, PROMPT_ROWS)
            xp = _attn_prompt(sb_bias[j], qb, kb, vb, n_seq, seq)
            kfs, vfs, qbs, _, _ = _qkv(hs, gmix, wq, n_dec)
            xs = _attn_decode(page_table, qbs.astype(F32), sb_bias[j], cache_k, cache_v, j)
            k_p.append(kt.reshape(n_seq, N_HEADS, HEAD_DIM, seq))
            v_p.append(vt.reshape(n_seq, N_HEADS, HEAD_DIM, seq))
            k_s.append(kfs.reshape(n_dec, 1, N_HEADS, HEAD_DIM))
            v_s.append(vfs.reshape(n_dec, 1, N_HEADS, HEAD_DIM))
            mix = (w_o[j], zero_bias, row(norm_ffn[i]))
            wr = jnp.zeros((D_MODEL, LANES), F32).at[:, :N_EXPERTS].set(w_router[j])
            moe = (wr, w_moe_gate[j], w_moe_up[j], w_moe_down[j])
            hp = _mixer_moe_routed(hp, xp, *mix, *moe, pp, *ple, final, PROMPT_ROWS)
            hs = _mixer_moe(hs, xs, *mix, *moe, ps, *ple, final, n_dec)
    seq_major = lambda rows: jnp.transpose(jnp.stack(rows), (0, 1, 4, 2, 3))
    return (hp.reshape(n_seq, seq, D_MODEL), hs.reshape(n_dec, 1, D_MODEL),
            seq_major(k_p), seq_major(v_p), jnp.stack(k_s), jnp.stack(v_s),
            jnp.stack(conv_p), jnp.stack(conv_s))
```

```python
import functools

import jax
import jax.numpy as jnp
from jax import lax
from jax.experimental import pallas as pl
from jax.experimental.pallas import tpu as pltpu

D_MODEL = 1024
N_HEADS = 16
HEAD_DIM = 64
SB_SCALE = HEAD_DIM ** -0.5
CONV_WIDTH = 31
D_FF = 2816
N_EXPERTS = 8
D_PLE = 256
RMS_EPS = 1e-6
LN_EPS = 1e-5
PAGE_SIZE = 128

LANES = 128
HALO_ROWS = 32
PROMPT_ROWS = 512
CONV_ROWS = 256
ATTN_ROWS = 256
FF_CHUNK = 256
MOE_CHUNKS = 2
DEC_PAGES = 8
EXPERT_ROWS = 512
BF16_SUBLANES = 16
CAST_BLOCK_BYTES = 4 * 1024 * 1024
VMEM_LIMIT = 56 * 1024 * 1024
LOG2E = 1.4426950408889634
MASKED = -1e30
META_E1, META_E2, META_R1, META_R2, META_W1, META_W2 = range(6)

BF16 = jnp.bfloat16
F32 = jnp.float32


def _params(n_axes):
    return pltpu.CompilerParams(dimension_semantics=("arbitrary",) * n_axes,
                                vmem_limit_bytes=VMEM_LIMIT)


def _rms(x, g):
    ms = jnp.mean(x * x, axis=-1, keepdims=True)
    return x * lax.rsqrt(ms + RMS_EPS) * g


def _sigmoid(x):
    return 1.0 / (1.0 + jnp.exp(-x))


def _const_spec(shape):
    zeros = (0,) * len(shape)
    return pl.BlockSpec(shape, lambda *_: zeros)


def _cast_kernel(x_ref, o_ref):
    o_ref[...] = x_ref[...].astype(BF16)


def _to_bf16(w):
    rows, cols = w.shape[-2:]
    w3 = w.reshape(-1, rows, cols)
    step = max(r for r in range(BF16_SUBLANES, rows + 1, BF16_SUBLANES)
               if rows % r == 0 and r * cols * 4 <= CAST_BLOCK_BYTES)
    spec = pl.BlockSpec((None, step, cols), lambda a, r: (a, r, 0))
    out = pl.pallas_call(
        _cast_kernel,
        grid=(w3.shape[0], rows // step),
        in_specs=[spec],
        out_specs=spec,
        out_shape=jax.ShapeDtypeStruct(w3.shape, BF16),
        compiler_params=_params(2),
        name="cast_bf16",
    )(w3)
    return out.reshape(w.shape)


def _glu_kernel(h_ref, g_ref, w_ref, b_ref, o_ref):
    u = _rms(h_ref[...], g_ref[...]).astype(BF16)
    a = jnp.dot(u, w_ref[...], preferred_element_type=F32) + b_ref[...]
    o_ref[...] = a[:, :D_MODEL] * _sigmoid(a[:, D_MODEL:])


def _glu(h, g, w, b, tm):
    m = h.shape[0]
    return pl.pallas_call(
        _glu_kernel,
        grid=(m // tm,),
        in_specs=[pl.BlockSpec((tm, D_MODEL), lambda i: (i, 0)),
                  _const_spec((1, D_MODEL)),
                  _const_spec((D_MODEL, 2 * D_MODEL)),
                  _const_spec((1, 2 * D_MODEL))],
        out_specs=pl.BlockSpec((tm, D_MODEL), lambda i: (i, 0)),
        out_shape=jax.ShapeDtypeStruct((m, D_MODEL), F32),
        compiler_params=_params(1),
        name="glu",
    )(h, g, w, b)


def _ln_swish(y, lg, lb):
    mu = jnp.mean(y, axis=-1, keepdims=True)
    yc = y - mu
    var = jnp.mean(yc * yc, axis=-1, keepdims=True)
    yn = yc * lax.rsqrt(var + LN_EPS) * lg + lb
    return yn * _sigmoid(yn)


def _dwconv_kernel(halo_ref, x_ref, wd_ref, bd_ref, lg_ref, lb_ref, o_ref, buf, sh, ybuf):
    tq = CONV_ROWS
    i = pl.program_id(1)
    buf[0:HALO_ROWS, :] = jnp.where(i > 0, halo_ref[...], 0.0)
    buf[HALO_ROWS:HALO_ROWS + tq, :] = x_ref[...]
    base = HALO_ROWS - (CONV_WIDTH - 1)
    n_taps = [len(range(r, CONV_WIDTH, 8)) for r in range(8)]
    for r in range(8):
        n = tq + 8 * (n_taps[r] - 1)
        sh[r, 0:n, :] = buf[base + r:base + r + n, :]

    rb = 64

    def col_body(c, carry):
        cs = pl.ds(pl.multiple_of(c * LANES, LANES), LANES)
        for blk in range(tq // rb):
            acc = jnp.zeros((rb, LANES), F32) + bd_ref[:, cs]
            for r in range(8):
                for a in range(n_taps[r]):
                    w = 8 * a + r
                    lo = 8 * a + blk * rb
                    acc = acc + sh[r, lo:lo + rb, cs] * wd_ref[w:w + 1, cs]
            ybuf[blk * rb:(blk + 1) * rb, cs] = acc
        return carry

    lax.fori_loop(0, D_MODEL // LANES, col_body, 0)
    o_ref[...] = _ln_swish(ybuf[...], lg_ref[...], lb_ref[...]).astype(BF16)


def _dwconv_prompt(glu, n_seq, seq, wd, bd, lg, lb):
    tq = CONV_ROWS
    nq = seq // tq
    halo_per_blk = tq // HALO_ROWS

    def halo_map(b, i):
        return (jnp.maximum(b * (seq // HALO_ROWS) + i * halo_per_blk - 1, 0), 0)

    return pl.pallas_call(
        _dwconv_kernel,
        grid=(n_seq, nq),
        in_specs=[pl.BlockSpec((HALO_ROWS, D_MODEL), halo_map),
                  pl.BlockSpec((tq, D_MODEL), lambda b, i: (b * nq + i, 0)),
                  _const_spec((CONV_WIDTH, D_MODEL)),
                  _const_spec((1, D_MODEL)),
                  _const_spec((1, D_MODEL)),
                  _const_spec((1, D_MODEL))],
        out_specs=pl.BlockSpec((tq, D_MODEL), lambda b, i: (b * nq + i, 0)),
        out_shape=jax.ShapeDtypeStruct((n_seq * seq, D_MODEL), BF16),
        scratch_shapes=[pltpu.VMEM((HALO_ROWS + tq, D_MODEL), F32),
                        pltpu.VMEM((8, tq + 24, D_MODEL), F32),
                        pltpu.VMEM((tq, D_MODEL), F32)],
        compiler_params=_params(2),
        name="dwconv_prompt",
    )(glu, glu, wd, bd, lg, lb)


def _dwconv_sample_kernel(hist_ref, x_ref, wd_ref, bd_ref, lg_ref, lb_ref, o_ref):
    acc = x_ref[...] * wd_ref[CONV_WIDTH - 1:CONV_WIDTH, :] + bd_ref[...]
    for w in range(CONV_WIDTH - 1):
        acc = acc + hist_ref[w] * wd_ref[w:w + 1, :]
    o_ref[...] = _ln_swish(acc, lg_ref[...], lb_ref[...]).astype(BF16)


def _dwconv_sample(hist_t, glu, wd, bd, lg, lb):
    n = glu.shape[0]
    return pl.pallas_call(
        _dwconv_sample_kernel,
        grid=(1,),
        in_specs=[_const_spec((CONV_WIDTH - 1, n, D_MODEL)),
                  _const_spec((n, D_MODEL)),
                  _const_spec((CONV_WIDTH, D_MODEL)),
                  _const_spec((1, D_MODEL)),
                  _const_spec((1, D_MODEL)),
                  _const_spec((1, D_MODEL))],
        out_specs=_const_spec((n, D_MODEL)),
        out_shape=jax.ShapeDtypeStruct((n, D_MODEL), BF16),
        compiler_params=_params(1),
        name="dwconv_sample",
    )(hist_t, glu, wd, bd, lg, lb)


def _qkv_kernel(h_ref, g_ref, w_ref, kf_ref, vf_ref, qb_ref, kb_ref, vb_ref):
    u = _rms(h_ref[...], g_ref[...]).astype(BF16)
    qkv = jnp.dot(u, w_ref[...], preferred_element_type=F32)
    k = qkv[:, D_MODEL:2 * D_MODEL]
    v = qkv[:, 2 * D_MODEL:]
    kf_ref[...] = k
    vf_ref[...] = v
    qb_ref[...] = (qkv[:, :D_MODEL] * (SB_SCALE * LOG2E)).astype(BF16)
    kb_ref[...] = k.astype(BF16)
    vb_ref[...] = v.astype(BF16)


def _qkv(h, g, w, tm):
    m = h.shape[0]
    row = pl.BlockSpec((tm, D_MODEL), lambda i: (i, 0))
    return pl.pallas_call(
        _qkv_kernel,
        grid=(m // tm,),
        in_specs=[row, _const_spec((1, D_MODEL)), _const_spec((D_MODEL, 3 * D_MODEL))],
        out_specs=[row] * 5,
        out_shape=[jax.ShapeDtypeStruct((m, D_MODEL), F32)] * 2
        + [jax.ShapeDtypeStruct((m, D_MODEL), BF16)] * 3,
        compiler_params=_params(1),
        name="qkv",
    )(h, g, w)


def _qkv_seq_kernel(h_ref, g_ref, w_ref, wt_ref, kt_ref, vt_ref, qb_ref, ktb_ref, vb_ref):
    u = _rms(h_ref[...], g_ref[...]).astype(BF16)
    qkv = jnp.dot(u, w_ref[...], preferred_element_type=F32)
    qb_ref[...] = (qkv[:, :D_MODEL] * (SB_SCALE * LOG2E)).astype(BF16)
    vb_ref[...] = qkv[:, 2 * D_MODEL:].astype(BF16)
    kvt = lax.dot_general(wt_ref[...], u, (((1,), (1,)), ((), ())), preferred_element_type=F32)
    kt_ref[...] = kvt[:D_MODEL]
    vt_ref[...] = kvt[D_MODEL:]
    ktb_ref[...] = kvt[:D_MODEL].astype(BF16)


def _qkv_seq(h, g, w, n_seq, tm):
    m = h.shape[0]
    seq = m // n_seq
    blocks = seq // tm
    wt = jnp.swapaxes(w[:, D_MODEL:], 0, 1)
    row = pl.BlockSpec((tm, D_MODEL), lambda i: (i, 0))
    col = pl.BlockSpec((None, D_MODEL, tm), lambda i: (i // blocks, 0, i % blocks))
    return pl.pallas_call(
        _qkv_seq_kernel,
        grid=(m // tm,),
        in_specs=[row, _const_spec((1, D_MODEL)), _const_spec((D_MODEL, 3 * D_MODEL)),
                  _const_spec((2 * D_MODEL, D_MODEL))],
        out_specs=[col, col, row, col, row],
        out_shape=[jax.ShapeDtypeStruct((n_seq, D_MODEL, seq), F32)] * 2
        + [jax.ShapeDtypeStruct((m, D_MODEL), BF16),
           jax.ShapeDtypeStruct((n_seq, D_MODEL, seq), BF16),
           jax.ShapeDtypeStruct((m, D_MODEL), BF16)],
        compiler_params=_params(1),
        name="qkv_seq",
    )(h, g, w, wt)


def _softplus2(y):
    neg_abs = lax.bitcast_convert_type(
        lax.bitcast_convert_type(y, jnp.uint32) | jnp.uint32(0x80000000), F32)
    return jnp.maximum(y, 0.0) + jnp.log(1.0 + jnp.exp2(neg_abs)) * LOG2E


def _attn_kernel(bias_ref, q_ref, k_ref, v_ref, t_ref, o_ref, *scratch):
    t = ATTN_ROWS
    nq = q_ref.shape[0] // t
    n_tiles = nq * (nq + 1) // 2
    hp = pl.program_id(1)
    heads = (scratch[:7], scratch[7:])
    lane = lax.broadcasted_iota(jnp.int32, (1, LANES), 1)
    first = lane < HEAD_DIM
    q = q_ref[...]
    row = lax.broadcasted_iota(jnp.int32, (t, t), 0)
    col = lax.broadcasted_iota(jnp.int32, (t, t), 1)
    for hh, (qm, bm, o_acc, c_acc, y_buf, sp_buf, a_buf) in enumerate(heads):
        keep = first if hh == 0 else jnp.logical_not(first)
        qm[...] = jnp.where(keep, q, jnp.zeros_like(q))
        bias = jnp.full((t, t), bias_ref[2 * hp + hh], F32)
        bm[0] = bias
        bm[1] = jnp.where(col < row, bias, MASKED)
        bm[2] = jnp.full((t, t), MASKED, F32)
        o_acc[...] = jnp.zeros_like(o_acc)
        c_acc[...] = jnp.zeros_like(c_acc)
        y_buf[...] = jnp.full(y_buf.shape, MASKED, F32)
        sp_buf[...] = jnp.zeros_like(sp_buf)
        a_buf[...] = jnp.zeros_like(a_buf)

    def rows(blk):
        return pl.ds(pl.multiple_of(blk * t, t), t)

    def body(f, tiles):
        (i0, j0), _, (i2, _), (i3, j3) = tiles
        kind = jnp.where(f >= n_tiles, 2, jnp.where(j0 == i0, 1, 0))
        y0, y1, y2 = lax.rem(f, 3), lax.rem(f + 2, 3), lax.rem(f + 1, 3)
        even, odd = lax.rem(f, 2), lax.rem(f + 1, 2)
        kblk = k_ref[:, rows(j0)]
        vblk = v_ref[rows(j3), :]
        for qm, bm, o_acc, c_acc, y_buf, sp_buf, a_buf in heads:
            o_acc[rows(i3), :] += jnp.dot(a_buf[odd], vblk, preferred_element_type=F32)
            cum = jnp.dot(sp_buf[even], t_ref[...], preferred_element_type=F32)
            c = c_acc[i2]
            a_buf[even] = jnp.exp2(y_buf[y2] - cum - c).astype(BF16)
            c_acc[i2] = c + cum[:, 0:1]
            sp_buf[odd] = _softplus2(y_buf[y1]).astype(BF16)
            y_buf[y0] = jnp.dot(qm[rows(i0), :], kblk, preferred_element_type=F32) + bm[kind]
        wrap = j0 == 0
        i_next = jnp.minimum(jnp.where(wrap, i0 + 1, i0), nq - 1)
        j_next = jnp.minimum(jnp.where(wrap, i0 + 1, j0 - 1), nq - 1)
        return ((i_next, j_next),) + tiles[:3]

    zero = jnp.int32(0)
    lax.fori_loop(0, n_tiles + 3, body, ((zero, zero),) * 4)
    o_ref[...] = jnp.where(first, heads[0][2][...], heads[1][2][...]).astype(BF16)


def _attn_prompt(sb_bias, qb, ktb, vb, n_seq, seq):
    t = ATTN_ROWS
    nq = seq // t
    tmat = (jnp.arange(t)[:, None] >= jnp.arange(t)[None, :]).astype(BF16)
    seq_spec = pl.BlockSpec((None, seq, LANES), lambda b, hp: (b, 0, hp))
    per_head = [pltpu.VMEM((seq, LANES), BF16),
                pltpu.VMEM((3, t, t), F32),
                pltpu.VMEM((seq, LANES), F32),
                pltpu.VMEM((nq, t, 1), F32),
                pltpu.VMEM((3, t, t), F32),
                pltpu.VMEM((2, t, t), BF16),
                pltpu.VMEM((2, t, t), BF16)]
    out = pl.pallas_call(
        _attn_kernel,
        grid=(n_seq, N_HEADS // 2),
        in_specs=[pl.BlockSpec(memory_space=pltpu.SMEM),
                  seq_spec,
                  pl.BlockSpec((None, LANES, seq), lambda b, hp: (b, hp, 0)),
                  seq_spec,
                  _const_spec((t, t))],
        out_specs=seq_spec,
        out_shape=jax.ShapeDtypeStruct((n_seq, seq, D_MODEL), BF16),
        scratch_shapes=per_head * 2,
        compiler_params=_params(2),
        name="attn_prompt",
    )(sb_bias * LOG2E, qb.reshape(n_seq, seq, D_MODEL), ktb, vb.reshape(n_seq, seq, D_MODEL), tmat)
    return out.reshape(n_seq * seq, D_MODEL)


def _decode_kernel(pt_ref, qbd_ref, bias_ref, t_ref, *refs):
    k_refs = refs[:DEC_PAGES]
    v_refs = refs[DEC_PAGES:2 * DEC_PAGES]
    o_ref, acc, carry = refs[2 * DEC_PAGES:]
    g = pl.program_id(1)

    @pl.when(g == 0)
    def _():
        acc[...] = jnp.zeros_like(acc)
        carry[...] = jnp.zeros_like(carry)

    qbd = qbd_ref[...]
    for s in range(DEC_PAGES):
        y = jnp.dot(qbd, k_refs[s][...].astype(BF16), preferred_element_type=F32) + bias_ref[...]
        sp = _softplus2(y)
        cum = jnp.dot(sp.astype(BF16), t_ref[...], preferred_element_type=F32)
        a = jnp.exp2(y - cum - carry[...])
        carry[...] += cum[:, 0:1]
        a_rows = jnp.broadcast_to(a[:, None, :], (N_HEADS, HEAD_DIM, PAGE_SIZE))
        acc[...] += v_refs[s][...] * a_rows.reshape(D_MODEL, PAGE_SIZE)

    @pl.when(g == pl.num_programs(1) - 1)
    def _():
        ones = jnp.ones((8, PAGE_SIZE), F32)
        sums = lax.dot_general(ones, acc[...], (((1,), (1,)), ((), ())),
                               preferred_element_type=F32, precision=lax.Precision.HIGHEST)
        o_ref[...] = sums[0:1, :]


def _attn_decode(page_table, q_scaled, sb_bias, cache_k, cache_v, layer):
    n, n_pages = page_table.shape
    n_layers, pool = cache_k.shape[:2]
    by_position = lambda c: jnp.transpose(c, (0, 1, 3, 4, 2)).reshape(
        n_layers, pool, D_MODEL, PAGE_SIZE)
    cache_k, cache_v = by_position(cache_k), by_position(cache_v)
    head_of_col = jnp.arange(D_MODEL) // HEAD_DIM
    own = jnp.arange(N_HEADS)[:, None] == head_of_col[None, :]
    qbd = jnp.where(own[None], q_scaled[:, None, :], 0.0).astype(BF16)
    bias = jnp.broadcast_to((sb_bias * LOG2E)[:, None], (N_HEADS, PAGE_SIZE))
    tmat = (jnp.arange(PAGE_SIZE)[:, None] >= jnp.arange(PAGE_SIZE)[None, :]).astype(BF16)
    steps = n_pages // DEC_PAGES

    def page_spec(s):
        def index_map(b, g, pt):
            return (layer, pt[b, n_pages - 1 - g * DEC_PAGES - s], 0, 0)
        return pl.BlockSpec((None, None, D_MODEL, PAGE_SIZE), index_map)

    grid_spec = pltpu.PrefetchScalarGridSpec(
        num_scalar_prefetch=1,
        grid=(n, steps),
        in_specs=[pl.BlockSpec((None, N_HEADS, D_MODEL), lambda b, g, pt: (b, 0, 0)),
                  pl.BlockSpec((N_HEADS, PAGE_SIZE), lambda b, g, pt: (0, 0)),
                  pl.BlockSpec((PAGE_SIZE, PAGE_SIZE), lambda b, g, pt: (0, 0))]
        + [page_spec(s) for s in range(DEC_PAGES)] * 2,
        out_specs=pl.BlockSpec((None, 1, D_MODEL), lambda b, g, pt: (b, 0, 0)),
        scratch_shapes=[pltpu.VMEM((D_MODEL, PAGE_SIZE), F32), pltpu.VMEM((N_HEADS, 1), F32)],
    )
    out = pl.pallas_call(
        _decode_kernel,
        grid_spec=grid_spec,
        out_shape=jax.ShapeDtypeStruct((n, 1, D_MODEL), F32),
        compiler_params=_params(2),
        name="attn_decode",
    )(page_table, qbd, bias, tmat, *([cache_k] * DEC_PAGES), *([cache_v] * DEC_PAGES))
    return out.reshape(n, D_MODEL)


def _ple_tail(hres, p_ref, gple_ref, wgate_ref, wproj_ref, gfin_ref, final):
    r = _rms(hres, gple_ref[...]).astype(BF16)
    gate = _sigmoid(jnp.dot(r, wgate_ref[...], preferred_element_type=F32))
    proj = jnp.dot(p_ref[...].astype(BF16), wproj_ref[...], preferred_element_type=F32)
    out = hres + gate * proj
    if final:
        out = _rms(out, gfin_ref[...])
    return out


def _mix_in(h_ref, x_ref, wmix_ref, bmix_ref):
    return h_ref[...] + jnp.dot(x_ref[...].astype(BF16), wmix_ref[...],
                                preferred_element_type=F32) + bmix_ref[...]


def _silu_mul(g, u):
    return (g * _sigmoid(g) * u).astype(BF16)


def _mixer_dense_kernel(h_ref, x_ref, wmix_ref, bmix_ref, gffn_ref, wg_ref, wu_ref, wd_ref,
                        p_ref, gple_ref, wgate_ref, wproj_ref, gfin_ref, o_ref, *, final):
    h1 = _mix_in(h_ref, x_ref, wmix_ref, bmix_ref)
    f = _rms(h1, gffn_ref[...]).astype(BF16)
    acc = h1
    for c in range(D_FF // FF_CHUNK):
        cs = slice(c * FF_CHUNK, (c + 1) * FF_CHUNK)
        g = jnp.dot(f, wg_ref[:, cs], preferred_element_type=F32)
        u = jnp.dot(f, wu_ref[:, cs], preferred_element_type=F32)
        acc = acc + jnp.dot(_silu_mul(g, u), wd_ref[cs, :], preferred_element_type=F32)
    o_ref[...] = _ple_tail(acc, p_ref, gple_ref, wgate_ref, wproj_ref, gfin_ref, final)


def _single(shape):
    zeros = (0,) * len(shape)
    return pl.BlockSpec(shape, lambda *_: zeros, pipeline_mode=pl.Buffered(1))


def _mixer_dense(h, x, wmix, bmix, gffn, wg, wu, wd, p, gple, wgate, wproj, gfin, final, tm):
    m = h.shape[0]
    row = lambda width: pl.BlockSpec((tm, width), lambda i: (i, 0))
    return pl.pallas_call(
        functools.partial(_mixer_dense_kernel, final=final),
        grid=(m // tm,),
        in_specs=[row(D_MODEL), row(D_MODEL),
                  _single((D_MODEL, D_MODEL)), _single((1, D_MODEL)), _single((1, D_MODEL)),
                  _single((D_MODEL, D_FF)), _single((D_MODEL, D_FF)), _single((D_FF, D_MODEL)),
                  row(D_PLE), _single((1, D_MODEL)),
                  _single((D_MODEL, D_MODEL)), _single((D_PLE, D_MODEL)), _single((1, D_MODEL))],
        out_specs=row(D_MODEL),
        out_shape=jax.ShapeDtypeStruct((m, D_MODEL), F32),
        compiler_params=_params(1),
        name="mixer_dense",
    )(h, x, wmix, bmix, gffn, wg, wu, wd, p, gple, wgate, wproj, gfin)


def _top2(logits):
    lane = lax.broadcasted_iota(jnp.int32, logits.shape, 1).astype(F32)
    neg = -jnp.inf
    lg = jnp.where(lane < N_EXPERTS, logits, neg)
    m1 = jnp.max(lg, axis=1, keepdims=True)
    i1 = jnp.min(jnp.where(lg == m1, lane, float(LANES)), axis=1, keepdims=True)
    lg2 = jnp.where(lane == i1, neg, lg)
    m2 = jnp.max(lg2, axis=1, keepdims=True)
    i2 = jnp.min(jnp.where(lg2 == m2, lane, float(LANES)), axis=1, keepdims=True)
    t = jnp.exp(m2 - m1)
    return lane, i1, i2, 1.0 / (1.0 + t), t / (1.0 + t)


def _router_logits(f, wr_ref):
    return jnp.dot(f, wr_ref[...], preferred_element_type=F32, precision=lax.Precision.HIGHEST)


def _mixer_moe_kernel(h_ref, x_ref, wmix_ref, bmix_ref, gffn_ref, wr_ref, wg_ref, wu_ref, wd_ref,
                      p_ref, gple_ref, wgate_ref, wproj_ref, gfin_ref, o_ref,
                      f_scr, gate_scr, acc_scr, *, final):
    e = pl.program_id(1)
    c = pl.program_id(2)

    @pl.when((e == 0) & (c == 0))
    def _():
        h1 = _mix_in(h_ref, x_ref, wmix_ref, bmix_ref)
        f = _rms(h1, gffn_ref[...])
        lane, i1, i2, w1, w2 = _top2(_router_logits(f, wr_ref))
        gate_scr[...] = jnp.where(lane == i1, w1, 0.0) + jnp.where(lane == i2, w2, 0.0)
        f_scr[...] = f.astype(BF16)
        acc_scr[...] = h1

    f = f_scr[...]
    g = jnp.dot(f, wg_ref[...], preferred_element_type=F32)
    u = jnp.dot(f, wu_ref[...], preferred_element_type=F32)
    y = jnp.dot(_silu_mul(g, u), wd_ref[...], preferred_element_type=F32)
    gate = gate_scr[...]
    lane = lax.broadcasted_iota(jnp.int32, gate.shape, 1)
    ge = jnp.sum(jnp.where(lane == e, gate, 0.0), axis=1, keepdims=True)
    acc_scr[...] += ge * y

    @pl.when((e == pl.num_programs(1) - 1) & (c == pl.num_programs(2) - 1))
    def _():
        o_ref[...] = _ple_tail(acc_scr[...], p_ref, gple_ref, wgate_ref, wproj_ref, gfin_ref, final)


def _mixer_moe(h, x, wmix, bmix, gffn, wr, wg, wu, wd, p, gple, wgate, wproj, gfin, final, tm):
    m = h.shape[0]
    fc = D_FF // MOE_CHUNKS
    row = lambda width: pl.BlockSpec((tm, width), lambda i, e, c: (i, 0))
    return pl.pallas_call(
        functools.partial(_mixer_moe_kernel, final=final),
        grid=(m // tm, N_EXPERTS, MOE_CHUNKS),
        in_specs=[row(D_MODEL), row(D_MODEL),
                  _single((D_MODEL, D_MODEL)), _single((1, D_MODEL)), _single((1, D_MODEL)),
                  _single((D_MODEL, LANES)),
                  pl.BlockSpec((None, D_MODEL, fc), lambda i, e, c: (e, 0, c)),
                  pl.BlockSpec((None, D_MODEL, fc), lambda i, e, c: (e, 0, c)),
                  pl.BlockSpec((None, fc, D_MODEL), lambda i, e, c: (e, c, 0)),
                  row(D_PLE), _single((1, D_MODEL)),
                  _single((D_MODEL, D_MODEL)), _single((D_PLE, D_MODEL)), _single((1, D_MODEL))],
        out_specs=row(D_MODEL),
        out_shape=jax.ShapeDtypeStruct((m, D_MODEL), F32),
        scratch_shapes=[pltpu.VMEM((tm, D_MODEL), BF16),
                        pltpu.VMEM((tm, LANES), F32),
                        pltpu.VMEM((tm, D_MODEL), F32)],
        compiler_params=_params(3),
        name="mixer_moe",
    )(h, x, wmix, bmix, gffn, wr, wg, wu, wd, p, gple, wgate, wproj, gfin)


def _moe_route_kernel(h_ref, x_ref, wmix_ref, bmix_ref, gffn_ref, wr_ref, ltri_ref,
                      h1_ref, f_ref, meta_ref, cnt_ref):
    h1 = _mix_in(h_ref, x_ref, wmix_ref, bmix_ref)
    f = _rms(h1, gffn_ref[...])
    lane, i1, i2, w1, w2 = _top2(_router_logits(f, wr_ref))
    sel = jnp.where(lane == i1, 1.0, 0.0) + jnp.where(lane == i2, 1.0, 0.0)
    rank = jnp.dot(ltri_ref[...], sel.astype(BF16), preferred_element_type=F32)
    r1 = jnp.sum(jnp.where(lane == i1, rank, 0.0), axis=1, keepdims=True)
    r2 = jnp.sum(jnp.where(lane == i2, rank, 0.0), axis=1, keepdims=True)
    meta = jnp.zeros_like(rank)
    for pos, val in ((META_E1, i1), (META_E2, i2), (META_R1, r1), (META_R2, r2),
                     (META_W1, w1), (META_W2, w2)):
        meta = jnp.where(lane == float(pos), val, meta)
    h1_ref[...] = h1
    f_ref[...] = f
    meta_ref[...] = meta
    cnt_ref[...] = jnp.sum(sel, axis=0, keepdims=True)


def _moe_route(h, x, wmix, bmix, gffn, wr, tm):
    m = h.shape[0]
    ltri = (jnp.arange(tm)[:, None] > jnp.arange(tm)[None, :]).astype(BF16)
    row = lambda width: pl.BlockSpec((tm, width), lambda i: (i, 0))
    return pl.pallas_call(
        _moe_route_kernel,
        grid=(m // tm,),
        in_specs=[row(D_MODEL), row(D_MODEL),
                  _single((D_MODEL, D_MODEL)), _single((1, D_MODEL)), _single((1, D_MODEL)),
                  _single((D_MODEL, LANES)), _single((tm, tm))],
        out_specs=[row(D_MODEL), row(D_MODEL), row(LANES),
                   pl.BlockSpec((None, 1, LANES), lambda i: (i, 0, 0))],
        out_shape=[jax.ShapeDtypeStruct((m, D_MODEL), F32), jax.ShapeDtypeStruct((m, D_MODEL), F32),
                   jax.ShapeDtypeStruct((m, LANES), F32),
                   jax.ShapeDtypeStruct((m // tm, 1, LANES), F32)],
        compiler_params=_params(1),
        name="moe_route",
    )(h, x, wmix, bmix, gffn, wr, ltri)


def _route_plan(meta, cnt, tm):
    m = meta.shape[0]
    n_rows = 2 * m + N_EXPERTS * EXPERT_ROWS
    expert = meta[:, META_E1:META_E2 + 1].astype(jnp.int32)
    rank = meta[:, META_R1:META_R2 + 1].astype(jnp.int32)
    cnt = cnt[:, 0, :N_EXPERTS].astype(jnp.int32)
    total = jnp.sum(cnt, axis=0)
    region = (total + EXPERT_ROWS - 1) // EXPERT_ROWS * EXPERT_ROWS
    region_end = jnp.cumsum(region)
    base = (region_end - region)[None, :] + jnp.cumsum(cnt, axis=0) - cnt
    base_rows = jnp.repeat(base, tm, axis=0)[:, None, :]
    hit = expert[:, :, None] == jnp.arange(N_EXPERTS)[None, None, :]
    dest = (jnp.sum(jnp.where(hit, base_rows, 0), axis=2) + rank).reshape(-1)
    starts = jnp.arange(n_rows // EXPERT_ROWS) * EXPERT_ROWS
    tile_expert = jnp.minimum(jnp.sum(starts[:, None] >= region_end[None, :], axis=1),
                              N_EXPERTS - 1).astype(jnp.int32)
    active = (starts < region_end[-1]).astype(jnp.int32)
    return dest.astype(jnp.int32), tile_expert, active, n_rows


def _row_copies(idx_ref, n, make):
    def start(m, carry):
        for k in range(2):
            make(m, k, idx_ref[2 * m + k]).start()
        return carry

    def wait(m, carry):
        for k in range(2):
            make(m, k, idx_ref[2 * m + k]).wait()
        return carry

    lax.fori_loop(0, n, start, 0)
    lax.fori_loop(0, n, wait, 0)


def _moe_scatter_kernel(dest_ref, f_ref, zeros_ref, xs_ref, sem):
    del zeros_ref

    def make(m, k, d):
        return pltpu.make_async_copy(f_ref.at[pl.ds(m, 1)], xs_ref.at[pl.ds(d, 1)], sem)

    _row_copies(dest_ref, f_ref.shape[0], make)


def _moe_scatter(dest, f, n_rows, tm):
    m = f.shape[0]
    return pl.pallas_call(
        _moe_scatter_kernel,
        grid=(m // tm,),
        in_specs=[pl.BlockSpec((2 * tm,), lambda i: (i,), memory_space=pltpu.SMEM),
                  pl.BlockSpec((tm, D_MODEL), lambda i: (i, 0)),
                  pl.BlockSpec(memory_space=pl.ANY)],
        out_specs=pl.BlockSpec(memory_space=pl.ANY),
        out_shape=jax.ShapeDtypeStruct((n_rows, D_MODEL), F32),
        scratch_shapes=[pltpu.SemaphoreType.DMA(())],
        input_output_aliases={2: 0},
        compiler_params=_params(1),
        name="moe_scatter",
    )(dest, f, jnp.zeros((n_rows, D_MODEL), F32))


def _moe_expert_kernel(expert_ref, active_ref, x_ref, wg_ref, wu_ref, wd_ref, o_ref):
    del expert_ref
    k = pl.program_id(0)

    @pl.when(active_ref[k] != 0)
    def _():
        x = x_ref[...].astype(BF16)
        acc = None
        for c in range(D_FF // FF_CHUNK):
            cs = slice(c * FF_CHUNK, (c + 1) * FF_CHUNK)
            g = jnp.dot(x, wg_ref[:, cs], preferred_element_type=F32)
            u = jnp.dot(x, wu_ref[:, cs], preferred_element_type=F32)
            y = jnp.dot(_silu_mul(g, u), wd_ref[cs, :], preferred_element_type=F32)
            acc = y if acc is None else acc + y
        o_ref[...] = acc

    @pl.when(active_ref[k] == 0)
    def _():
        o_ref[...] = jnp.zeros_like(o_ref)


def _moe_experts(tile_expert, active, xs, wg, wu, wd):
    n_rows = xs.shape[0]
    rows = pl.BlockSpec((EXPERT_ROWS, D_MODEL), lambda k, te, ac: (k, 0))
    grid_spec = pltpu.PrefetchScalarGridSpec(
        num_scalar_prefetch=2,
        grid=(n_rows // EXPERT_ROWS,),
        in_specs=[rows,
                  pl.BlockSpec((None, D_MODEL, D_FF), lambda k, te, ac: (te[k], 0, 0)),
                  pl.BlockSpec((None, D_MODEL, D_FF), lambda k, te, ac: (te[k], 0, 0)),
                  pl.BlockSpec((None, D_FF, D_MODEL), lambda k, te, ac: (te[k], 0, 0))],
        out_specs=rows,
    )
    return pl.pallas_call(
        _moe_expert_kernel,
        grid_spec=grid_spec,
        out_shape=jax.ShapeDtypeStruct((n_rows, D_MODEL), F32),
        compiler_params=_params(1),
        name="moe_experts",
    )(tile_expert, active, xs, wg, wu, wd)


def _moe_combine_kernel(dest_ref, ys_ref, h1_ref, meta_ref, p_ref, gple_ref, wgate_ref, wproj_ref,
                        gfin_ref, o_ref, ybuf, sem, *, final):
    def make(m, k, d):
        return pltpu.make_async_copy(ys_ref.at[pl.ds(d, 1)], ybuf.at[k, pl.ds(m, 1)], sem)

    _row_copies(dest_ref, h1_ref.shape[0], make)
    meta = meta_ref[...]
    lane = lax.broadcasted_iota(jnp.int32, meta.shape, 1)
    w1 = jnp.sum(jnp.where(lane == META_W1, meta, 0.0), axis=1, keepdims=True)
    w2 = jnp.sum(jnp.where(lane == META_W2, meta, 0.0), axis=1, keepdims=True)
    hres = h1_ref[...] + (w1 * ybuf[0] + w2 * ybuf[1])
    o_ref[...] = _ple_tail(hres, p_ref, gple_ref, wgate_ref, wproj_ref, gfin_ref, final)


def _moe_combine(dest, ys, h1, meta, p, gple, wgate, wproj, gfin, final, tm):
    m = h1.shape[0]
    row = lambda width: pl.BlockSpec((tm, width), lambda i: (i, 0))
    return pl.pallas_call(
        functools.partial(_moe_combine_kernel, final=final),
        grid=(m // tm,),
        in_specs=[pl.BlockSpec((2 * tm,), lambda i: (i,), memory_space=pltpu.SMEM),
                  pl.BlockSpec(memory_space=pl.ANY),
                  row(D_MODEL), row(LANES), row(D_PLE), _single((1, D_MODEL)),
                  _single((D_MODEL, D_MODEL)), _single((D_PLE, D_MODEL)), _single((1, D_MODEL))],
        out_specs=row(D_MODEL),
        out_shape=jax.ShapeDtypeStruct((m, D_MODEL), F32),
        scratch_shapes=[pltpu.VMEM((2, tm, D_MODEL), F32), pltpu.SemaphoreType.DMA(())],
        compiler_params=_params(1),
        name="moe_combine",
    )(dest, ys, h1, meta, p, gple, wgate, wproj, gfin)


def _mixer_moe_routed(h, x, wmix, bmix, gffn, wr, wg, wu, wd, p, gple, wgate, wproj, gfin, final, tm):
    h1, f, meta, cnt = _moe_route(h, x, wmix, bmix, gffn, wr, tm)
    dest, tile_expert, active, n_rows = _route_plan(meta, cnt, tm)
    xs = _moe_scatter(dest, f, n_rows, tm)
    ys = _moe_experts(tile_expert, active, xs, wg, wu, wd)
    return _moe_combine(dest, ys, h1, meta, p, gple, wgate, wproj, gfin, final, tm)


def kernel(x_prompt, x_sample, cache_k, cache_v, state_conv, page_table, p_prompt, p_sample, norm_mix, w_pw1, b_pw1, w_dw, b_dw, ln_conv_g, ln_conv_b, w_pw2, b_pw2, w_qkv, w_o, sb_bias, norm_ffn, w_ffn_gate, w_ffn_up, w_ffn_down, w_router, w_moe_gate, w_moe_up, w_moe_down, norm_ple, w_ple_gate, w_ple_proj, norm_final):
    n_seq, seq, _ = x_prompt.shape
    n_dec = x_sample.shape[0]
    depth = norm_mix.shape[0]
    mp = n_seq * seq
    hp = x_prompt.reshape(mp, D_MODEL)
    hs = x_sample.reshape(n_dec, D_MODEL)
    row = lambda a: a.reshape(1, -1)
    zero_bias = jnp.zeros((1, D_MODEL), F32)
    gfin = row(norm_final)
    w_pw1, w_pw2, w_qkv, w_o = map(_to_bf16, (w_pw1, w_pw2, w_qkv, w_o))
    w_ffn_gate, w_ffn_up, w_ffn_down = map(_to_bf16, (w_ffn_gate, w_ffn_up, w_ffn_down))
    w_moe_gate, w_moe_up, w_moe_down = map(_to_bf16, (w_moe_gate, w_moe_up, w_moe_down))
    w_ple_gate, w_ple_proj = map(_to_bf16, (w_ple_gate, w_ple_proj))
    last_rows = CONV_WIDTH - 1

    k_p, v_p, k_s, v_s, conv_p, conv_s = [], [], [], [], [], []
    for i in range(depth):
        j = i // 2
        final = i == depth - 1
        gmix = row(norm_mix[i])
        ple = (row(norm_ple[i]), w_ple_gate[i], w_ple_proj[i], gfin)
        pp = p_prompt[i].reshape(mp, D_PLE)
        ps = p_sample[i].reshape(n_dec, D_PLE)
        if i % 2 == 0:
            w1 = w_pw1[j]
            b1 = row(b_pw1[j])
            conv_w = (w_dw[j], row(b_dw[j]), row(ln_conv_g[j]), row(ln_conv_b[j]))
            glu_p = _glu(hp, gmix, w1, b1, PROMPT_ROWS)
            glu_s = _glu(hs, gmix, w1, b1, n_dec)
            xp = _dwconv_prompt(glu_p, n_seq, seq, *conv_w)
            xs = _dwconv_sample(jnp.swapaxes(state_conv[j], 0, 1), glu_s, *conv_w)
            conv_p.append(glu_p.reshape(n_seq, seq, D_MODEL)[:, seq - last_rows:])
            conv_s.append(jnp.concatenate([state_conv[j][:, 1:], glu_s[:, None, :]], axis=1))
            mix = (w_pw2[j], row(b_pw2[j]), row(norm_ffn[i]))
            ffn = (w_ffn_gate[j], w_ffn_up[j], w_ffn_down[j])
            hp = _mixer_dense(hp, xp, *mix, *ffn, pp, *ple, final, PROMPT_ROWS)
            hs = _mixer_dense(hs, xs, *mix, *ffn, ps, *ple, final, n_dec)
        else:
            wq = w_qkv[j]
            kt, vt, qb, ktb, vb = _qkv_seq(hp, gmix, wq, n_seq, PROMPT_ROWS)
            xp = _attn_prompt(sb_bias[j], qb, ktb, vb, n_seq, seq)
            kfs, vfs, qbs, _, _ = _qkv(hs, gmix, wq, n_dec)
            xs = _attn_decode(page_table, qbs.astype(F32), sb_bias[j], cache_k, cache_v, j)
            k_p.append(kt.reshape(n_seq, N_HEADS, HEAD_DIM, seq))
            v_p.append(vt.reshape(n_seq, N_HEADS, HEAD_DIM, seq))
            k_s.append(kfs.reshape(n_dec, 1, N_HEADS, HEAD_DIM))
            v_s.append(vfs.reshape(n_dec, 1, N_HEADS, HEAD_DIM))
            mix = (w_o[j], zero_bias, row(norm_ffn[i]))
            wr = jnp.zeros((D_MODEL, LANES), F32).at[:, :N_EXPERTS].set(w_router[j])
            moe = (wr, w_moe_gate[j], w_moe_up[j], w_moe_down[j])
            hp = _mixer_moe_routed(hp, xp, *mix, *moe, pp, *ple, final, PROMPT_ROWS)
            hs = _mixer_moe(hs, xs, *mix, *moe, ps, *ple, final, n_dec)
    seq_major = lambda rows: jnp.transpose(jnp.stack(rows), (0, 1, 4, 2, 3))
    return (hp.reshape(n_seq, seq, D_MODEL), hs.reshape(n_dec, 1, D_MODEL),
            seq_major(k_p), seq_major(v_p), jnp.stack(k_s), jnp.stack(v_s),
            jnp.stack(conv_p), jnp.stack(conv_s))
```

```python
import functools

import jax
import jax.numpy as jnp
from jax import lax
from jax.experimental import pallas as pl
from jax.experimental.pallas import tpu as pltpu

D_MODEL = 1024
N_HEADS = 16
HEAD_DIM = 64
SB_SCALE = HEAD_DIM ** -0.5
CONV_WIDTH = 31
D_FF = 2816
N_EXPERTS = 8
D_PLE = 256
RMS_EPS = 1e-6
LN_EPS = 1e-5
PAGE_SIZE = 128

LANES = 128
HALO_ROWS = 32
PROMPT_ROWS = 512
CONV_ROWS = 256
ATTN_ROWS = 256
FF_CHUNK = 256
MOE_CHUNKS = 2
DEC_PAGES = 8
EXPERT_ROWS = 512
BF16_SUBLANES = 16
CAST_BLOCK_BYTES = 4 * 1024 * 1024
VMEM_LIMIT = 56 * 1024 * 1024
LOG2E = 1.4426950408889634
MASKED = -1e30
META_E1, META_E2, META_R1, META_R2, META_W1, META_W2 = range(6)

BF16 = jnp.bfloat16
F32 = jnp.float32


def _params(n_axes):
    return pltpu.CompilerParams(dimension_semantics=("arbitrary",) * n_axes,
                                vmem_limit_bytes=VMEM_LIMIT)


def _rms(x, g):
    ms = jnp.mean(x * x, axis=-1, keepdims=True)
    return x * lax.rsqrt(ms + RMS_EPS) * g


def _sigmoid(x):
    return 1.0 / (1.0 + jnp.exp(-x))


def _const_spec(shape):
    zeros = (0,) * len(shape)
    return pl.BlockSpec(shape, lambda *_: zeros)


def _cast_kernel(x_ref, o_ref):
    o_ref[...] = x_ref[...].astype(BF16)


def _to_bf16(w):
    rows, cols = w.shape[-2:]
    w3 = w.reshape(-1, rows, cols)
    step = max(r for r in range(BF16_SUBLANES, rows + 1, BF16_SUBLANES)
               if rows % r == 0 and r * cols * 4 <= CAST_BLOCK_BYTES)
    spec = pl.BlockSpec((None, step, cols), lambda a, r: (a, r, 0))
    out = pl.pallas_call(
        _cast_kernel,
        grid=(w3.shape[0], rows // step),
        in_specs=[spec],
        out_specs=spec,
        out_shape=jax.ShapeDtypeStruct(w3.shape, BF16),
        compiler_params=_params(2),
        name="cast_bf16",
    )(w3)
    return out.reshape(w.shape)


def _glu_kernel(h_ref, g_ref, w_ref, b_ref, o_ref):
    u = _rms(h_ref[...], g_ref[...]).astype(BF16)
    a = jnp.dot(u, w_ref[...], preferred_element_type=F32) + b_ref[...]
    o_ref[...] = a[:, :D_MODEL] * _sigmoid(a[:, D_MODEL:])


def _glu(h, g, w, b, tm):
    m = h.shape[0]
    return pl.pallas_call(
        _glu_kernel,
        grid=(m // tm,),
        in_specs=[pl.BlockSpec((tm, D_MODEL), lambda i: (i, 0)),
                  _const_spec((1, D_MODEL)),
                  _const_spec((D_MODEL, 2 * D_MODEL)),
                  _const_spec((1, 2 * D_MODEL))],
        out_specs=pl.BlockSpec((tm, D_MODEL), lambda i: (i, 0)),
        out_shape=jax.ShapeDtypeStruct((m, D_MODEL), F32),
        compiler_params=_params(1),
        name="glu",
    )(h, g, w, b)


def _ln_swish(y, lg, lb):
    mu = jnp.mean(y, axis=-1, keepdims=True)
    yc = y - mu
    var = jnp.mean(yc * yc, axis=-1, keepdims=True)
    yn = yc * lax.rsqrt(var + LN_EPS) * lg + lb
    return yn * _sigmoid(yn)


def _dwconv_kernel(halo_ref, x_ref, wd_ref, bd_ref, lg_ref, lb_ref, o_ref, buf, sh, ybuf):
    tq = CONV_ROWS
    i = pl.program_id(1)
    buf[0:HALO_ROWS, :] = jnp.where(i > 0, halo_ref[...], 0.0)
    buf[HALO_ROWS:HALO_ROWS + tq, :] = x_ref[...]
    base = HALO_ROWS - (CONV_WIDTH - 1)
    n_taps = [len(range(r, CONV_WIDTH, 8)) for r in range(8)]
    for r in range(8):
        n = tq + 8 * (n_taps[r] - 1)
        sh[r, 0:n, :] = buf[base + r:base + r + n, :]

    rb = 64

    def col_body(c, carry):
        cs = pl.ds(pl.multiple_of(c * LANES, LANES), LANES)
        for blk in range(tq // rb):
            acc = jnp.zeros((rb, LANES), F32) + bd_ref[:, cs]
            for r in range(8):
                for a in range(n_taps[r]):
                    w = 8 * a + r
                    lo = 8 * a + blk * rb
                    acc = acc + sh[r, lo:lo + rb, cs] * wd_ref[w:w + 1, cs]
            ybuf[blk * rb:(blk + 1) * rb, cs] = acc
        return carry

    lax.fori_loop(0, D_MODEL // LANES, col_body, 0)
    o_ref[...] = _ln_swish(ybuf[...], lg_ref[...], lb_ref[...]).astype(BF16)


def _dwconv_prompt(glu, n_seq, seq, wd, bd, lg, lb):
    tq = CONV_ROWS
    nq = seq // tq
    halo_per_blk = tq // HALO_ROWS

    def halo_map(b, i):
        return (jnp.maximum(b * (seq // HALO_ROWS) + i * halo_per_blk - 1, 0), 0)

    return pl.pallas_call(
        _dwconv_kernel,
        grid=(n_seq, nq),
        in_specs=[pl.BlockSpec((HALO_ROWS, D_MODEL), halo_map),
                  pl.BlockSpec((tq, D_MODEL), lambda b, i: (b * nq + i, 0)),
                  _const_spec((CONV_WIDTH, D_MODEL)),
                  _const_spec((1, D_MODEL)),
                  _const_spec((1, D_MODEL)),
                  _const_spec((1, D_MODEL))],
        out_specs=pl.BlockSpec((tq, D_MODEL), lambda b, i: (b * nq + i, 0)),
        out_shape=jax.ShapeDtypeStruct((n_seq * seq, D_MODEL), BF16),
        scratch_shapes=[pltpu.VMEM((HALO_ROWS + tq, D_MODEL), F32),
                        pltpu.VMEM((8, tq + 24, D_MODEL), F32),
                        pltpu.VMEM((tq, D_MODEL), F32)],
        compiler_params=_params(2),
        name="dwconv_prompt",
    )(glu, glu, wd, bd, lg, lb)


def _dwconv_sample_kernel(hist_ref, x_ref, wd_ref, bd_ref, lg_ref, lb_ref, o_ref):
    acc = x_ref[...] * wd_ref[CONV_WIDTH - 1:CONV_WIDTH, :] + bd_ref[...]
    for w in range(CONV_WIDTH - 1):
        acc = acc + hist_ref[w] * wd_ref[w:w + 1, :]
    o_ref[...] = _ln_swish(acc, lg_ref[...], lb_ref[...]).astype(BF16)


def _dwconv_sample(hist_t, glu, wd, bd, lg, lb):
    n = glu.shape[0]
    return pl.pallas_call(
        _dwconv_sample_kernel,
        grid=(1,),
        in_specs=[_const_spec((CONV_WIDTH - 1, n, D_MODEL)),
                  _const_spec((n, D_MODEL)),
                  _const_spec((CONV_WIDTH, D_MODEL)),
                  _const_spec((1, D_MODEL)),
                  _const_spec((1, D_MODEL)),
                  _const_spec((1, D_MODEL))],
        out_specs=_const_spec((n, D_MODEL)),
        out_shape=jax.ShapeDtypeStruct((n, D_MODEL), BF16),
        compiler_params=_params(1),
        name="dwconv_sample",
    )(hist_t, glu, wd, bd, lg, lb)


def _qkv_kernel(h_ref, g_ref, w_ref, kf_ref, vf_ref, qb_ref, kb_ref, vb_ref):
    u = _rms(h_ref[...], g_ref[...]).astype(BF16)
    qkv = jnp.dot(u, w_ref[...], preferred_element_type=F32)
    k = qkv[:, D_MODEL:2 * D_MODEL]
    v = qkv[:, 2 * D_MODEL:]
    kf_ref[...] = k
    vf_ref[...] = v
    qb_ref[...] = (qkv[:, :D_MODEL] * (SB_SCALE * LOG2E)).astype(BF16)
    kb_ref[...] = k.astype(BF16)
    vb_ref[...] = v.astype(BF16)


def _qkv(h, g, w, tm):
    m = h.shape[0]
    row = pl.BlockSpec((tm, D_MODEL), lambda i: (i, 0))
    return pl.pallas_call(
        _qkv_kernel,
        grid=(m // tm,),
        in_specs=[row, _const_spec((1, D_MODEL)), _const_spec((D_MODEL, 3 * D_MODEL))],
        out_specs=[row] * 5,
        out_shape=[jax.ShapeDtypeStruct((m, D_MODEL), F32)] * 2
        + [jax.ShapeDtypeStruct((m, D_MODEL), BF16)] * 3,
        compiler_params=_params(1),
        name="qkv",
    )(h, g, w)


def _qkv_seq_kernel(h_ref, g_ref, w_ref, wt_ref, kt_ref, vt_ref, qb_ref, ktb_ref, vb_ref):
    u = _rms(h_ref[...], g_ref[...]).astype(BF16)
    qkv = jnp.dot(u, w_ref[...], preferred_element_type=F32)
    qb_ref[...] = (qkv[:, :D_MODEL] * (SB_SCALE * LOG2E)).astype(BF16)
    vb_ref[...] = qkv[:, 2 * D_MODEL:].astype(BF16)
    kvt = lax.dot_general(wt_ref[...], u, (((1,), (1,)), ((), ())), preferred_element_type=F32)
    kt_ref[...] = kvt[:D_MODEL]
    vt_ref[...] = kvt[D_MODEL:]
    ktb_ref[...] = kvt[:D_MODEL].astype(BF16)


def _qkv_seq(h, g, w, n_seq, tm):
    m = h.shape[0]
    seq = m // n_seq
    blocks = seq // tm
    wt = jnp.swapaxes(w[:, D_MODEL:], 0, 1)
    row = pl.BlockSpec((tm, D_MODEL), lambda i: (i, 0))
    col = pl.BlockSpec((None, D_MODEL, tm), lambda i: (i // blocks, 0, i % blocks))
    return pl.pallas_call(
        _qkv_seq_kernel,
        grid=(m // tm,),
        in_specs=[row, _const_spec((1, D_MODEL)), _const_spec((D_MODEL, 3 * D_MODEL)),
                  _const_spec((2 * D_MODEL, D_MODEL))],
        out_specs=[col, col, row, col, row],
        out_shape=[jax.ShapeDtypeStruct((n_seq, D_MODEL, seq), F32)] * 2
        + [jax.ShapeDtypeStruct((m, D_MODEL), BF16),
           jax.ShapeDtypeStruct((n_seq, D_MODEL, seq), BF16),
           jax.ShapeDtypeStruct((m, D_MODEL), BF16)],
        compiler_params=_params(1),
        name="qkv_seq",
    )(h, g, w, wt)


def _softplus2(y):
    neg_abs = lax.bitcast_convert_type(
        lax.bitcast_convert_type(y, jnp.uint32) | jnp.uint32(0x80000000), F32)
    return jnp.maximum(y, 0.0) + jnp.log(1.0 + jnp.exp2(neg_abs)) * LOG2E


def _attn_kernel(bias_ref, q_ref, k_ref, v_ref, t_ref, o_ref, *scratch):
    t = ATTN_ROWS
    nq = q_ref.shape[0] // t
    n_tiles = nq * (nq + 1) // 2
    hp = pl.program_id(1)
    heads = (scratch[:7], scratch[7:])
    lane = lax.broadcasted_iota(jnp.int32, (1, LANES), 1)
    first = lane < HEAD_DIM
    q = q_ref[...]
    row = lax.broadcasted_iota(jnp.int32, (t, t), 0)
    col = lax.broadcasted_iota(jnp.int32, (t, t), 1)
    for hh, (qm, bm, o_acc, c_acc, y_buf, sp_buf, a_buf) in enumerate(heads):
        keep = first if hh == 0 else jnp.logical_not(first)
        qm[...] = jnp.where(keep, q, jnp.zeros_like(q))
        bias = jnp.full((t, t), bias_ref[2 * hp + hh], F32)
        bm[0] = bias
        bm[1] = jnp.where(col < row, bias, MASKED)
        bm[2] = jnp.full((t, t), MASKED, F32)
        o_acc[...] = jnp.zeros_like(o_acc)
        c_acc[...] = jnp.zeros_like(c_acc)
        y_buf[...] = jnp.full(y_buf.shape, MASKED, F32)
        sp_buf[...] = jnp.zeros_like(sp_buf)
        a_buf[...] = jnp.zeros_like(a_buf)

    def rows(blk):
        return pl.ds(pl.multiple_of(blk * t, t), t)

    def body(f, tiles):
        (i0, j0), _, (i2, _), (i3, j3) = tiles
        kind = jnp.where(f >= n_tiles, 2, jnp.where(j0 == i0, 1, 0))
        y0, y1, y2 = lax.rem(f, 3), lax.rem(f + 2, 3), lax.rem(f + 1, 3)
        even, odd = lax.rem(f, 2), lax.rem(f + 1, 2)
        kblk = k_ref[:, rows(j0)]
        vblk = v_ref[rows(j3), :]
        for qm, bm, o_acc, c_acc, y_buf, sp_buf, a_buf in heads:
            o_acc[rows(i3), :] += jnp.dot(a_buf[odd], vblk, preferred_element_type=F32)
            cum = jnp.dot(sp_buf[even], t_ref[...], preferred_element_type=F32)
            c = c_acc[i2]
            a_buf[even] = jnp.exp2(y_buf[y2] - cum - c).astype(BF16)
            c_acc[i2] = c + cum[:, 0:1]
            sp_buf[odd] = _softplus2(y_buf[y1]).astype(BF16)
            y_buf[y0] = jnp.dot(qm[rows(i0), :], kblk, preferred_element_type=F32) + bm[kind]
        wrap = j0 == 0
        i_next = jnp.minimum(jnp.where(wrap, i0 + 1, i0), nq - 1)
        j_next = jnp.minimum(jnp.where(wrap, i0 + 1, j0 - 1), nq - 1)
        return ((i_next, j_next),) + tiles[:3]

    zero = jnp.int32(0)
    lax.fori_loop(0, n_tiles + 3, body, ((zero, zero),) * 4)
    o_ref[...] = jnp.where(first, heads[0][2][...], heads[1][2][...]).astype(BF16)


def _attn_prompt(sb_bias, qb, ktb, vb, n_seq, seq):
    t = ATTN_ROWS
    nq = seq // t
    tmat = (jnp.arange(t)[:, None] >= jnp.arange(t)[None, :]).astype(BF16)
    seq_spec = pl.BlockSpec((None, seq, LANES), lambda b, hp: (b, 0, hp))
    per_head = [pltpu.VMEM((seq, LANES), BF16),
                pltpu.VMEM((3, t, t), F32),
                pltpu.VMEM((seq, LANES), F32),
                pltpu.VMEM((nq, t, 1), F32),
                pltpu.VMEM((3, t, t), F32),
                pltpu.VMEM((2, t, t), BF16),
                pltpu.VMEM((2, t, t), BF16)]
    out = pl.pallas_call(
        _attn_kernel,
        grid=(n_seq, N_HEADS // 2),
        in_specs=[pl.BlockSpec(memory_space=pltpu.SMEM),
                  seq_spec,
                  pl.BlockSpec((None, LANES, seq), lambda b, hp: (b, hp, 0)),
                  seq_spec,
                  _const_spec((t, t))],
        out_specs=seq_spec,
        out_shape=jax.ShapeDtypeStruct((n_seq, seq, D_MODEL), BF16),
        scratch_shapes=per_head * 2,
        compiler_params=_params(2),
        name="attn_prompt",
    )(sb_bias * LOG2E, qb.reshape(n_seq, seq, D_MODEL), ktb, vb.reshape(n_seq, seq, D_MODEL), tmat)
    return out.reshape(n_seq * seq, D_MODEL)


def _decode_kernel(pt_ref, qbd_ref, bias_ref, t_ref, *refs):
    k_refs = refs[:DEC_PAGES]
    v_refs = refs[DEC_PAGES:2 * DEC_PAGES]
    o_ref, acc, carry = refs[2 * DEC_PAGES:]
    g = pl.program_id(1)

    @pl.when(g == 0)
    def _():
        acc[...] = jnp.zeros_like(acc)
        carry[...] = jnp.zeros_like(carry)

    qbd = qbd_ref[...]
    for s in range(DEC_PAGES):
        y = jnp.dot(qbd, k_refs[s][...].astype(BF16), preferred_element_type=F32) + bias_ref[...]
        sp = _softplus2(y)
        cum = jnp.dot(sp.astype(BF16), t_ref[...], preferred_element_type=F32)
        a = jnp.exp2(y - cum - carry[...])
        carry[...] += cum[:, 0:1]
        a_rows = jnp.broadcast_to(a[:, None, :], (N_HEADS, HEAD_DIM, PAGE_SIZE))
        acc[...] += v_refs[s][...] * a_rows.reshape(D_MODEL, PAGE_SIZE)

    @pl.when(g == pl.num_programs(1) - 1)
    def _():
        ones = jnp.ones((8, PAGE_SIZE), F32)
        sums = lax.dot_general(ones, acc[...], (((1,), (1,)), ((), ())),
                               preferred_element_type=F32, precision=lax.Precision.HIGHEST)
        o_ref[...] = sums[0:1, :]


def _attn_decode(page_table, q_scaled, sb_bias, cache_k, cache_v, layer):
    n, n_pages = page_table.shape
    n_layers, pool = cache_k.shape[:2]
    by_position = lambda c: jnp.transpose(c, (0, 1, 3, 4, 2)).reshape(
        n_layers, pool, D_MODEL, PAGE_SIZE)
    cache_k, cache_v = by_position(cache_k), by_position(cache_v)
    head_of_col = jnp.arange(D_MODEL) // HEAD_DIM
    own = jnp.arange(N_HEADS)[:, None] == head_of_col[None, :]
    qbd = jnp.where(own[None], q_scaled[:, None, :], 0.0).astype(BF16)
    bias = jnp.broadcast_to((sb_bias * LOG2E)[:, None], (N_HEADS, PAGE_SIZE))
    tmat = (jnp.arange(PAGE_SIZE)[:, None] >= jnp.arange(PAGE_SIZE)[None, :]).astype(BF16)
    steps = n_pages // DEC_PAGES

    def page_spec(s):
        def index_map(b, g, pt):
            return (layer, pt[b, n_pages - 1 - g * DEC_PAGES - s], 0, 0)
        return pl.BlockSpec((None, None, D_MODEL, PAGE_SIZE), index_map)

    grid_spec = pltpu.PrefetchScalarGridSpec(
        num_scalar_prefetch=1,
        grid=(n, steps),
        in_specs=[pl.BlockSpec((None, N_HEADS, D_MODEL), lambda b, g, pt: (b, 0, 0)),
                  pl.BlockSpec((N_HEADS, PAGE_SIZE), lambda b, g, pt: (0, 0)),
                  pl.BlockSpec((PAGE_SIZE, PAGE_SIZE), lambda b, g, pt: (0, 0))]
        + [page_spec(s) for s in range(DEC_PAGES)] * 2,
        out_specs=pl.BlockSpec((None, 1, D_MODEL), lambda b, g, pt: (b, 0, 0)),
        scratch_shapes=[pltpu.VMEM((D_MODEL, PAGE_SIZE), F32), pltpu.VMEM((N_HEADS, 1), F32)],
    )
    out = pl.pallas_call(
        _decode_kernel,
        grid_spec=grid_spec,
        out_shape=jax.ShapeDtypeStruct((n, 1, D_MODEL), F32),
        compiler_params=_params(2),
        name="attn_decode",
    )(page_table, qbd, bias, tmat, *([cache_k] * DEC_PAGES), *([cache_v] * DEC_PAGES))
    return out.reshape(n, D_MODEL)


def _ple_tail(hres, p, gple_ref, wgate_ref, wproj_ref, gfin_ref, final):
    r = _rms(hres, gple_ref[...]).astype(BF16)
    gate = _sigmoid(jnp.dot(r, wgate_ref[...], preferred_element_type=F32))
    proj = jnp.dot(p.astype(BF16), wproj_ref[...], preferred_element_type=F32)
    out = hres + gate * proj
    if final:
        out = _rms(out, gfin_ref[...])
    return out


def _mix_in(h_ref, x_ref, wmix_ref, bmix_ref):
    return h_ref[...] + jnp.dot(x_ref[...].astype(BF16), wmix_ref[...],
                                preferred_element_type=F32) + bmix_ref[...]


def _silu_mul(g, u):
    return (g * _sigmoid(g) * u).astype(BF16)


def _mixer_dense_kernel(h_ref, x_ref, wmix_ref, bmix_ref, gffn_ref, wg_ref, wu_ref, wd_ref,
                        p_ref, gple_ref, wgate_ref, wproj_ref, gfin_ref, o_ref, *, final):
    h1 = _mix_in(h_ref, x_ref, wmix_ref, bmix_ref)
    f = _rms(h1, gffn_ref[...]).astype(BF16)
    acc = h1
    for c in range(D_FF // FF_CHUNK):
        cs = slice(c * FF_CHUNK, (c + 1) * FF_CHUNK)
        g = jnp.dot(f, wg_ref[:, cs], preferred_element_type=F32)
        u = jnp.dot(f, wu_ref[:, cs], preferred_element_type=F32)
        acc = acc + jnp.dot(_silu_mul(g, u), wd_ref[cs, :], preferred_element_type=F32)
    o_ref[...] = _ple_tail(acc, p_ref[...], gple_ref, wgate_ref, wproj_ref, gfin_ref, final)


def _single(shape):
    zeros = (0,) * len(shape)
    return pl.BlockSpec(shape, lambda *_: zeros, pipeline_mode=pl.Buffered(1))


def _mixer_dense(h, x, wmix, bmix, gffn, wg, wu, wd, p, gple, wgate, wproj, gfin, final, tm):
    m = h.shape[0]
    row = lambda width: pl.BlockSpec((tm, width), lambda i: (i, 0))
    return pl.pallas_call(
        functools.partial(_mixer_dense_kernel, final=final),
        grid=(m // tm,),
        in_specs=[row(D_MODEL), row(D_MODEL),
                  _single((D_MODEL, D_MODEL)), _single((1, D_MODEL)), _single((1, D_MODEL)),
                  _single((D_MODEL, D_FF)), _single((D_MODEL, D_FF)), _single((D_FF, D_MODEL)),
                  row(D_PLE), _single((1, D_MODEL)),
                  _single((D_MODEL, D_MODEL)), _single((D_PLE, D_MODEL)), _single((1, D_MODEL))],
        out_specs=row(D_MODEL),
        out_shape=jax.ShapeDtypeStruct((m, D_MODEL), F32),
        compiler_params=_params(1),
        name="mixer_dense",
    )(h, x, wmix, bmix, gffn, wg, wu, wd, p, gple, wgate, wproj, gfin)


def _top2(logits):
    lane = lax.broadcasted_iota(jnp.int32, logits.shape, 1).astype(F32)
    neg = -jnp.inf
    lg = jnp.where(lane < N_EXPERTS, logits, neg)
    m1 = jnp.max(lg, axis=1, keepdims=True)
    i1 = jnp.min(jnp.where(lg == m1, lane, float(LANES)), axis=1, keepdims=True)
    lg2 = jnp.where(lane == i1, neg, lg)
    m2 = jnp.max(lg2, axis=1, keepdims=True)
    i2 = jnp.min(jnp.where(lg2 == m2, lane, float(LANES)), axis=1, keepdims=True)
    t = jnp.exp(m2 - m1)
    return lane, i1, i2, 1.0 / (1.0 + t), t / (1.0 + t)


def _router_logits(f, wr_ref):
    return jnp.dot(f, wr_ref[...], preferred_element_type=F32, precision=lax.Precision.HIGHEST)


def _mixer_moe_kernel(h_ref, x_ref, wmix_ref, bmix_ref, gffn_ref, wr_ref, wg_ref, wu_ref, wd_ref,
                      p_ref, gple_ref, wgate_ref, wproj_ref, gfin_ref, o_ref,
                      f_scr, gate_scr, acc_scr, *, final):
    e = pl.program_id(1)
    c = pl.program_id(2)

    @pl.when((e == 0) & (c == 0))
    def _():
        h1 = _mix_in(h_ref, x_ref, wmix_ref, bmix_ref)
        f = _rms(h1, gffn_ref[...])
        lane, i1, i2, w1, w2 = _top2(_router_logits(f, wr_ref))
        gate_scr[...] = jnp.where(lane == i1, w1, 0.0) + jnp.where(lane == i2, w2, 0.0)
        f_scr[...] = f.astype(BF16)
        acc_scr[...] = h1

    f = f_scr[...]
    g = jnp.dot(f, wg_ref[...], preferred_element_type=F32)
    u = jnp.dot(f, wu_ref[...], preferred_element_type=F32)
    y = jnp.dot(_silu_mul(g, u), wd_ref[...], preferred_element_type=F32)
    gate = gate_scr[...]
    lane = lax.broadcasted_iota(jnp.int32, gate.shape, 1)
    ge = jnp.sum(jnp.where(lane == e, gate, 0.0), axis=1, keepdims=True)
    acc_scr[...] += ge * y

    @pl.when((e == pl.num_programs(1) - 1) & (c == pl.num_programs(2) - 1))
    def _():
        o_ref[...] = _ple_tail(acc_scr[...], p_ref[...], gple_ref, wgate_ref, wproj_ref, gfin_ref,
                               final)


def _mixer_moe(h, x, wmix, bmix, gffn, wr, wg, wu, wd, p, gple, wgate, wproj, gfin, final, tm):
    m = h.shape[0]
    fc = D_FF // MOE_CHUNKS
    row = lambda width: pl.BlockSpec((tm, width), lambda i, e, c: (i, 0))
    return pl.pallas_call(
        functools.partial(_mixer_moe_kernel, final=final),
        grid=(m // tm, N_EXPERTS, MOE_CHUNKS),
        in_specs=[row(D_MODEL), row(D_MODEL),
                  _single((D_MODEL, D_MODEL)), _single((1, D_MODEL)), _single((1, D_MODEL)),
                  _single((D_MODEL, LANES)),
                  pl.BlockSpec((None, D_MODEL, fc), lambda i, e, c: (e, 0, c)),
                  pl.BlockSpec((None, D_MODEL, fc), lambda i, e, c: (e, 0, c)),
                  pl.BlockSpec((None, fc, D_MODEL), lambda i, e, c: (e, c, 0)),
                  row(D_PLE), _single((1, D_MODEL)),
                  _single((D_MODEL, D_MODEL)), _single((D_PLE, D_MODEL)), _single((1, D_MODEL))],
        out_specs=row(D_MODEL),
        out_shape=jax.ShapeDtypeStruct((m, D_MODEL), F32),
        scratch_shapes=[pltpu.VMEM((tm, D_MODEL), BF16),
                        pltpu.VMEM((tm, LANES), F32),
                        pltpu.VMEM((tm, D_MODEL), F32)],
        compiler_params=_params(3),
        name="mixer_moe",
    )(h, x, wmix, bmix, gffn, wr, wg, wu, wd, p, gple, wgate, wproj, gfin)


def _moe_route_kernel(h_ref, x_ref, wmix_ref, bmix_ref, gffn_ref, wr_ref, ltri_ref,
                      h1_ref, f_ref, meta_ref, cnt_ref):
    h1 = _mix_in(h_ref, x_ref, wmix_ref, bmix_ref)
    f = _rms(h1, gffn_ref[...])
    lane, i1, i2, w1, w2 = _top2(_router_logits(f, wr_ref))
    sel = jnp.where(lane == i1, 1.0, 0.0) + jnp.where(lane == i2, 1.0, 0.0)
    rank = jnp.dot(ltri_ref[...], sel.astype(BF16), preferred_element_type=F32)
    r1 = jnp.sum(jnp.where(lane == i1, rank, 0.0), axis=1, keepdims=True)
    r2 = jnp.sum(jnp.where(lane == i2, rank, 0.0), axis=1, keepdims=True)
    meta = jnp.zeros_like(rank)
    for pos, val in ((META_E1, i1), (META_E2, i2), (META_R1, r1), (META_R2, r2),
                     (META_W1, w1), (META_W2, w2)):
        meta = jnp.where(lane == float(pos), val, meta)
    h1_ref[...] = h1
    f_ref[...] = f
    meta_ref[...] = meta
    cnt_ref[...] = jnp.sum(sel, axis=0, keepdims=True)


def _moe_route(h, x, wmix, bmix, gffn, wr, tm):
    m = h.shape[0]
    ltri = (jnp.arange(tm)[:, None] > jnp.arange(tm)[None, :]).astype(BF16)
    row = lambda width: pl.BlockSpec((tm, width), lambda i: (i, 0))
    return pl.pallas_call(
        _moe_route_kernel,
        grid=(m // tm,),
        in_specs=[row(D_MODEL), row(D_MODEL),
                  _single((D_MODEL, D_MODEL)), _single((1, D_MODEL)), _single((1, D_MODEL)),
                  _single((D_MODEL, LANES)), _single((tm, tm))],
        out_specs=[row(D_MODEL), row(D_MODEL), row(LANES),
                   pl.BlockSpec((None, 1, LANES), lambda i: (i, 0, 0))],
        out_shape=[jax.ShapeDtypeStruct((m, D_MODEL), F32), jax.ShapeDtypeStruct((m, D_MODEL), F32),
                   jax.ShapeDtypeStruct((m, LANES), F32),
                   jax.ShapeDtypeStruct((m // tm, 1, LANES), F32)],
        compiler_params=_params(1),
        name="moe_route",
    )(h, x, wmix, bmix, gffn, wr, ltri)


def _route_plan(meta, cnt, tm):
    m = meta.shape[0]
    n_rows = 2 * m + N_EXPERTS * EXPERT_ROWS
    expert = meta[:, META_E1:META_E2 + 1].astype(jnp.int32)
    rank = meta[:, META_R1:META_R2 + 1].astype(jnp.int32)
    cnt = cnt[:, 0, :N_EXPERTS].astype(jnp.int32)
    total = jnp.sum(cnt, axis=0)
    region = (total + EXPERT_ROWS - 1) // EXPERT_ROWS * EXPERT_ROWS
    region_end = jnp.cumsum(region)
    base = (region_end - region)[None, :] + jnp.cumsum(cnt, axis=0) - cnt
    base_rows = jnp.repeat(base, tm, axis=0)[:, None, :]
    hit = expert[:, :, None] == jnp.arange(N_EXPERTS)[None, None, :]
    dest = (jnp.sum(jnp.where(hit, base_rows, 0), axis=2) + rank).reshape(-1)
    starts = jnp.arange(n_rows // EXPERT_ROWS) * EXPERT_ROWS
    tile_expert = jnp.minimum(jnp.sum(starts[:, None] >= region_end[None, :], axis=1),
                              N_EXPERTS - 1).astype(jnp.int32)
    active = (starts < region_end[-1]).astype(jnp.int32)
    return dest.astype(jnp.int32), tile_expert, active, n_rows


def _row_copies(idx_ref, lo, hi, make, wait):
    def body(m, carry):
        for k in range(2):
            copy = make(m, k, idx_ref[2 * m + k])
            if wait:
                copy.wait()
            else:
                copy.start()
        return carry

    lax.fori_loop(lo, hi, body, 0)


def _moe_scatter_kernel(dest_ref, f_ref, zeros_ref, xs_ref, sem):
    del zeros_ref

    def make(m, k, d):
        return pltpu.make_async_copy(f_ref.at[pl.ds(m, 1)], xs_ref.at[pl.ds(d, 1)], sem)

    _row_copies(dest_ref, 0, f_ref.shape[0], make, wait=False)
    _row_copies(dest_ref, 0, f_ref.shape[0], make, wait=True)


def _moe_scatter(dest, f, n_rows, tm):
    m = f.shape[0]
    return pl.pallas_call(
        _moe_scatter_kernel,
        grid=(m // tm,),
        in_specs=[pl.BlockSpec((2 * tm,), lambda i: (i,), memory_space=pltpu.SMEM),
                  pl.BlockSpec((tm, D_MODEL), lambda i: (i, 0)),
                  pl.BlockSpec(memory_space=pl.ANY)],
        out_specs=pl.BlockSpec(memory_space=pl.ANY),
        out_shape=jax.ShapeDtypeStruct((n_rows, D_MODEL), F32),
        scratch_shapes=[pltpu.SemaphoreType.DMA(())],
        input_output_aliases={2: 0},
        compiler_params=_params(1),
        name="moe_scatter",
    )(dest, f, jnp.zeros((n_rows, D_MODEL), F32))


def _moe_expert_kernel(expert_ref, active_ref, x_ref, wg_ref, wu_ref, wd_ref, o_ref):
    del expert_ref
    k = pl.program_id(0)

    @pl.when(active_ref[k] != 0)
    def _():
        x = x_ref[...].astype(BF16)
        acc = None
        for c in range(D_FF // FF_CHUNK):
            cs = slice(c * FF_CHUNK, (c + 1) * FF_CHUNK)
            g = jnp.dot(x, wg_ref[:, cs], preferred_element_type=F32)
            u = jnp.dot(x, wu_ref[:, cs], preferred_element_type=F32)
            y = jnp.dot(_silu_mul(g, u), wd_ref[cs, :], preferred_element_type=F32)
            acc = y if acc is None else acc + y
        o_ref[...] = acc

    @pl.when(active_ref[k] == 0)
    def _():
        o_ref[...] = jnp.zeros_like(o_ref)


def _moe_experts(tile_expert, active, xs, wg, wu, wd):
    n_rows = xs.shape[0]
    rows = pl.BlockSpec((EXPERT_ROWS, D_MODEL), lambda k, te, ac: (k, 0))
    grid_spec = pltpu.PrefetchScalarGridSpec(
        num_scalar_prefetch=2,
        grid=(n_rows // EXPERT_ROWS,),
        in_specs=[rows,
                  pl.BlockSpec((None, D_MODEL, D_FF), lambda k, te, ac: (te[k], 0, 0)),
                  pl.BlockSpec((None, D_MODEL, D_FF), lambda k, te, ac: (te[k], 0, 0)),
                  pl.BlockSpec((None, D_FF, D_MODEL), lambda k, te, ac: (te[k], 0, 0))],
        out_specs=rows,
    )
    return pl.pallas_call(
        _moe_expert_kernel,
        grid_spec=grid_spec,
        out_shape=jax.ShapeDtypeStruct((n_rows, D_MODEL), F32),
        compiler_params=_params(1),
        name="moe_experts",
    )(tile_expert, active, xs, wg, wu, wd)


def _moe_combine_kernel(dest_ref, ys_ref, h1_ref, meta_ref, p_ref, gple_ref, wgate_ref, wproj_ref,
                        gfin_ref, o_ref, ybuf, sems, *, final):
    half = h1_ref.shape[0] // 2
    spans = ((0, half), (half, 2 * half))

    def make(sem):
        def copy(m, k, d):
            return pltpu.make_async_copy(ys_ref.at[pl.ds(d, 1)], ybuf.at[k, pl.ds(m, 1)], sem)
        return copy

    for part, (lo, hi) in enumerate(spans):
        _row_copies(dest_ref, lo, hi, make(sems.at[part]), wait=False)
    for part, (lo, hi) in enumerate(spans):
        _row_copies(dest_ref, lo, hi, make(sems.at[part]), wait=True)
        meta = meta_ref[lo:hi, :]
        lane = lax.broadcasted_iota(jnp.int32, meta.shape, 1)
        w1 = jnp.sum(jnp.where(lane == META_W1, meta, 0.0), axis=1, keepdims=True)
        w2 = jnp.sum(jnp.where(lane == META_W2, meta, 0.0), axis=1, keepdims=True)
        hres = h1_ref[lo:hi, :] + (w1 * ybuf[0, lo:hi, :] + w2 * ybuf[1, lo:hi, :])
        o_ref[lo:hi, :] = _ple_tail(hres, p_ref[lo:hi, :], gple_ref, wgate_ref, wproj_ref, gfin_ref,
                                    final)


def _moe_combine(dest, ys, h1, meta, p, gple, wgate, wproj, gfin, final, tm):
    m = h1.shape[0]
    row = lambda width: pl.BlockSpec((tm, width), lambda i: (i, 0))
    return pl.pallas_call(
        functools.partial(_moe_combine_kernel, final=final),
        grid=(m // tm,),
        in_specs=[pl.BlockSpec((2 * tm,), lambda i: (i,), memory_space=pltpu.SMEM),
                  pl.BlockSpec(memory_space=pl.ANY),
                  row(D_MODEL), row(LANES), row(D_PLE), _single((1, D_MODEL)),
                  _single((D_MODEL, D_MODEL)), _single((D_PLE, D_MODEL)), _single((1, D_MODEL))],
        out_specs=row(D_MODEL),
        out_shape=jax.ShapeDtypeStruct((m, D_MODEL), F32),
        scratch_shapes=[pltpu.VMEM((2, tm, D_MODEL), F32), pltpu.SemaphoreType.DMA((2,))],
        compiler_params=_params(1),
        name="moe_combine",
    )(dest, ys, h1, meta, p, gple, wgate, wproj, gfin)


def _mixer_moe_routed(h, x, wmix, bmix, gffn, wr, wg, wu, wd, p, gple, wgate, wproj, gfin, final, tm):
    h1, f, meta, cnt = _moe_route(h, x, wmix, bmix, gffn, wr, tm)
    dest, tile_expert, active, n_rows = _route_plan(meta, cnt, tm)
    xs = _moe_scatter(dest, f, n_rows, tm)
    ys = _moe_experts(tile_expert, active, xs, wg, wu, wd)
    return _moe_combine(dest, ys, h1, meta, p, gple, wgate, wproj, gfin, final, tm)


def kernel(x_prompt, x_sample, cache_k, cache_v, state_conv, page_table, p_prompt, p_sample, norm_mix, w_pw1, b_pw1, w_dw, b_dw, ln_conv_g, ln_conv_b, w_pw2, b_pw2, w_qkv, w_o, sb_bias, norm_ffn, w_ffn_gate, w_ffn_up, w_ffn_down, w_router, w_moe_gate, w_moe_up, w_moe_down, norm_ple, w_ple_gate, w_ple_proj, norm_final):
    n_seq, seq, _ = x_prompt.shape
    n_dec = x_sample.shape[0]
    depth = norm_mix.shape[0]
    mp = n_seq * seq
    hp = x_prompt.reshape(mp, D_MODEL)
    hs = x_sample.reshape(n_dec, D_MODEL)
    row = lambda a: a.reshape(1, -1)
    zero_bias = jnp.zeros((1, D_MODEL), F32)
    gfin = row(norm_final)
    w_pw1, w_pw2, w_qkv, w_o = map(_to_bf16, (w_pw1, w_pw2, w_qkv, w_o))
    w_ffn_gate, w_ffn_up, w_ffn_down = map(_to_bf16, (w_ffn_gate, w_ffn_up, w_ffn_down))
    w_moe_gate, w_moe_up, w_moe_down = map(_to_bf16, (w_moe_gate, w_moe_up, w_moe_down))
    w_ple_gate, w_ple_proj = map(_to_bf16, (w_ple_gate, w_ple_proj))
    last_rows = CONV_WIDTH - 1

    k_p, v_p, k_s, v_s, conv_p, conv_s = [], [], [], [], [], []
    for i in range(depth):
        j = i // 2
        final = i == depth - 1
        gmix = row(norm_mix[i])
        ple = (row(norm_ple[i]), w_ple_gate[i], w_ple_proj[i], gfin)
        pp = p_prompt[i].reshape(mp, D_PLE)
        ps = p_sample[i].reshape(n_dec, D_PLE)
        if i % 2 == 0:
            w1 = w_pw1[j]
            b1 = row(b_pw1[j])
            conv_w = (w_dw[j], row(b_dw[j]), row(ln_conv_g[j]), row(ln_conv_b[j]))
            glu_p = _glu(hp, gmix, w1, b1, PROMPT_ROWS)
            glu_s = _glu(hs, gmix, w1, b1, n_dec)
            xp = _dwconv_prompt(glu_p, n_seq, seq, *conv_w)
            xs = _dwconv_sample(jnp.swapaxes(state_conv[j], 0, 1), glu_s, *conv_w)
            conv_p.append(glu_p.reshape(n_seq, seq, D_MODEL)[:, seq - last_rows:])
            conv_s.append(jnp.concatenate([state_conv[j][:, 1:], glu_s[:, None, :]], axis=1))
            mix = (w_pw2[j], row(b_pw2[j]), row(norm_ffn[i]))
            ffn = (w_ffn_gate[j], w_ffn_up[j], w_ffn_down[j])
            hp = _mixer_dense(hp, xp, *mix, *ffn, pp, *ple, final, PROMPT_ROWS)
            hs = _mixer_dense(hs, xs, *mix, *ffn, ps, *ple, final, n_dec)
        else:
            wq = w_qkv[j]
            kt, vt, qb, ktb, vb = _qkv_seq(hp, gmix, wq, n_seq, PROMPT_ROWS)
            xp = _attn_prompt(sb_bias[j], qb, ktb, vb, n_seq, seq)
            kfs, vfs, qbs, _, _ = _qkv(hs, gmix, wq, n_dec)
            xs = _attn_decode(page_table, qbs.astype(F32), sb_bias[j], cache_k, cache_v, j)
            k_p.append(kt.reshape(n_seq, N_HEADS, HEAD_DIM, seq))
            v_p.append(vt.reshape(n_seq, N_HEADS, HEAD_DIM, seq))
            k_s.append(kfs.reshape(n_dec, 1, N_HEADS, HEAD_DIM))
            v_s.append(vfs.reshape(n_dec, 1, N_HEADS, HEAD_DIM))
            mix = (w_o[j], zero_bias, row(norm_ffn[i]))
            wr = jnp.zeros((D_MODEL, LANES), F32).at[:, :N_EXPERTS].set(w_router[j])
            moe = (wr, w_moe_gate[j], w_moe_up[j], w_moe_down[j])
            hp = _mixer_moe_routed(hp, xp, *mix, *moe, pp, *ple, final, PROMPT_ROWS)
            hs = _mixer_moe(hs, xs, *mix, *moe, ps, *ple, final, n_dec)
    seq_major = lambda rows: jnp.transpose(jnp.stack(rows), (0, 1, 4, 2, 3))
    return (hp.reshape(n_seq, seq, D_MODEL), hs.reshape(n_dec, 1, D_MODEL),
            seq_major(k_p), seq_major(v_p), jnp.stack(k_s), jnp.stack(v_s),
            jnp.stack(conv_p), jnp.stack(conv_s))
```
